```python
import jax
import jax.numpy as jnp
from jax import lax
import numpy as np

D_MODEL = 1024
BATCH = 8
SEQ = 4096
DEPTH = 1
DEC_BATCH = 128
DEC_SEQ = 8
PAST_LEN = 8192
PAGE_SIZE = 128

RET_HEADS = 8
RET_DK = 64
RET_DV = 128
RET_CHUNK = 128
ROPE_BASE = 10000.0
NSA_HEADS = 16
NSA_GROUPS = 4
NSA_HPG = NSA_HEADS // NSA_GROUPS
NSA_HEAD_DIM = 64
CMP_LEN = 32
CMP_STRIDE = 16
CMP_HID = 128
SEL_BLOCK = 64
SEL_TOPK = 16
WINDOW = 512
NSA_QBLK = 16
N_GROUPS = 4
EXPERTS_PER_GROUP = 8
N_EXPERTS = N_GROUPS * EXPERTS_PER_GROUP
EXPERT_TOPK = 2
D_EXPERT = 512
MOE_BLOCK = 128
KV_COLS = 2 * NSA_GROUPS * NSA_HEAD_DIM
PROJ_SIZES = (RET_HEADS * RET_DK, RET_HEADS * RET_DK, RET_HEADS * RET_DV, RET_HEADS * RET_DV,
              NSA_HEADS * NSA_HEAD_DIM, KV_COLS, KV_COLS, KV_COLS, 3 * NSA_HEADS, 2 * D_MODEL)
D_IN = 2 * RET_HEADS * RET_DK + 2 * RET_HEADS * RET_DV + NSA_HEADS * NSA_HEAD_DIM + 3 * KV_COLS + 3 * NSA_HEADS + 2 * D_MODEL
RMS_EPS = 1e-6
NEG_INF = -1e30
FORCE_SCORE = 1e9
F32 = jnp.float32

kernel_name = 'hybrid_retention_nsa_hmoe_adaln_step'


def rmsnorm(x, g):
    xf = x.astype(F32)
    return xf * lax.rsqrt(jnp.mean(xf * xf, axis=-1, keepdims=True) + RMS_EPS) * g.astype(F32)


def adaln(c, w_ada, b_ada):
    m = jax.nn.silu(c.astype(F32)) @ w_ada + b_ada
    return [t[:, None, :] for t in jnp.split(m, 6, axis=-1)]


def modulate(h, shift, scale):
    return h * (1.0 + scale) + shift


def rope(x, pos):
    half = x.shape[-1] // 2
    inv_freq = ROPE_BASE ** (-jnp.arange(half, dtype=F32) / half)
    ang = pos.astype(F32)[:, None] * inv_freq[None, :]
    cos = jnp.cos(ang)[None, :, None, :]
    sin = jnp.sin(ang)[None, :, None, :]
    x1, x2 = x[..., :half], x[..., half:]
    return jnp.concatenate([x1 * cos - x2 * sin, x1 * sin + x2 * cos], axis=-1)


def mixer_inputs(h, w_in, pos):
    B, L, _ = h.shape
    cuts = np.cumsum(PROJ_SIZES)[:-1].tolist()
    rq, rk, rv, rg, nq, ckv, skv, wkv, ng, mg = jnp.split(h @ w_in, cuts, axis=-1)
    rq = rope(rq.reshape(B, L, RET_HEADS, RET_DK), pos)
    rk = rope(rk.reshape(B, L, RET_HEADS, RET_DK), pos) * (RET_DK ** -0.5)
    rv = rv.reshape(B, L, RET_HEADS, RET_DV)
    nq = nq.reshape(B, L, NSA_HEADS, NSA_HEAD_DIM)
    kv_shape = (B, L, 2, NSA_GROUPS, NSA_HEAD_DIM)
    ng = jax.nn.sigmoid(ng).reshape(B, L, 3, NSA_HEADS)
    return rq, rk, rv, rg, nq, ckv.reshape(kv_shape), skv.reshape(kv_shape), wkv.reshape(kv_shape), ng, mg


def retention_log_decay():
    return jnp.log1p(-jnp.exp2(-5.0 - jnp.arange(RET_HEADS, dtype=F32)))


def retention_chunk(s, qkv):
    q, k, v = qkv
    C = q.shape[1]
    lg = retention_log_decay()
    i = jnp.arange(C, dtype=F32)
    diff = i[:, None] - i[None, :]
    decay = jnp.where(diff[None] >= 0, jnp.exp(lg[:, None, None] * jnp.maximum(diff, 0.0)[None]), 0.0)
    qf, kf, vf = q.astype(F32), k.astype(F32), v.astype(F32)
    scores = jnp.einsum('bihd,bjhd->bhij', qf, kf) * decay
    o = jnp.einsum('bhij,bjhv->bihv', scores, vf)
    q_decay = jnp.exp(lg[None, :] * (i[:, None] + 1.0))
    o = o + jnp.einsum('bihd,bhdv->bihv', qf * q_decay[None, :, :, None], s)
    k_decay = jnp.exp(lg[None, :] * (C - 1.0 - i)[:, None])
    s_new = jnp.exp(lg * C)[None, :, None, None] * s + jnp.einsum('bjhd,bjhv->bhdv', kf * k_decay[None, :, :, None], vf)
    return s_new, o


def retention_prompt(q, k, v):
    B, S = q.shape[:2]
    n = S // RET_CHUNK

    def to_chunks(t):
        return jnp.moveaxis(t.reshape(B, n, RET_CHUNK, *t.shape[2:]), 1, 0)

    s0 = jnp.zeros((B, RET_HEADS, RET_DK, RET_DV), F32)
    s_fin, o = lax.scan(retention_chunk, s0, (to_chunks(q), to_chunks(k), to_chunks(v)))
    return jnp.moveaxis(o, 0, 1).reshape(B, S, RET_HEADS, RET_DV), s_fin


def cmp_segment_proj(rows, w1):
    B, N = rows.shape[:2]
    seg = rows.reshape(B, N // CMP_STRIDE, CMP_STRIDE, 2, NSA_GROUPS, NSA_HEAD_DIM).astype(F32)
    lo = jnp.einsum('bspcgd,cpdh->bscgh', seg, w1[:, :CMP_STRIDE])
    hi = jnp.einsum('bspcgd,cpdh->bscgh', seg, w1[:, CMP_STRIDE:])
    return lo, hi


def cmp_finish(lo, hi, w1, pe, w2):
    pe_term = jnp.einsum('cpd,cpdh->ch', pe, w1)
    hid = jax.nn.silu(lo[:, :-1] + hi[:, 1:] + pe_term[None, None, :, None, :])
    return jnp.einsum('bscgh,chd->bscgd', hid, w2)


def cmp_attend(q, q_pos, ckv_c):
    B, L = q.shape[:2]
    n_cmp = ckv_c.shape[1]
    qg = q.reshape(B, L, NSA_GROUPS, NSA_HPG, NSA_HEAD_DIM)
    s = jnp.einsum('blgrd,bcgd->blgrc', qg, ckv_c[:, :, 0]).astype(F32) * (NSA_HEAD_DIM ** -0.5)
    ends = CMP_STRIDE * jnp.arange(n_cmp) + CMP_LEN - 1
    mask = (ends[None, :] <= q_pos[:, None])[None, :, None, None, :]
    p = jax.nn.softmax(jnp.where(mask, s, NEG_INF), axis=-1) * mask
    o = jnp.einsum('blgrc,bcgd->blgrd', p, ckv_c[:, :, 1])
    return o.reshape(B, L, NSA_HEADS, NSA_HEAD_DIM), p.sum(axis=3)


def select_blocks(p_grp, q_pos, n_blocks):
    n_cmp = p_grp.shape[-1]
    c = jnp.arange(n_cmp)[:, None]
    n = jnp.arange(n_blocks)[None, :]
    overlap = jnp.clip(jnp.minimum(CMP_STRIDE * c + CMP_LEN, SEL_BLOCK * n + SEL_BLOCK)
                       - jnp.maximum(CMP_STRIDE * c, SEL_BLOCK * n), 0, None)
    score = jnp.einsum('blgc,cn->blgn', p_grp, overlap.astype(F32) / CMP_STRIDE)
    blk = jnp.arange(n_blocks)[None, :]
    qp = q_pos[:, None]
    valid = blk * SEL_BLOCK <= qp
    cur = qp // SEL_BLOCK
    forced = (blk == 0) | (blk == cur) | (blk == cur - 1)
    score = jnp.where(forced[None, :, None, :], FORCE_SCORE,
                      jnp.where(valid[None, :, None, :], score, -FORCE_SCORE))
    _, idx = lax.top_k(score, min(SEL_TOPK, n_blocks))
    ok = idx * SEL_BLOCK <= q_pos[None, :, None, None]
    return idx, ok


def sel_attend(q, q_pos, kb, idx, ok):
    B, L = q.shape[:2]
    qg = q.reshape(B, L, NSA_GROUPS, NSA_HPG, NSA_HEAD_DIM)
    kpos = idx[..., None] * SEL_BLOCK + jnp.arange(SEL_BLOCK)
    mask = ok[..., None] & (kpos <= q_pos[None, :, None, None, None])
    s = jnp.einsum('blgrd,blgnkd->blgrnk', qg, kb[..., 0, :]).astype(F32) * (NSA_HEAD_DIM ** -0.5)
    s = jnp.where(mask[:, :, :, None], s, NEG_INF)
    p = jax.nn.softmax(s.reshape(*s.shape[:4], -1), axis=-1).reshape(s.shape)
    o = jnp.einsum('blgrnk,blgnkd->blgrd', p, kb[..., 1, :])
    return o.reshape(B, L, NSA_HEADS, NSA_HEAD_DIM)


def win_attend(q, q_pos, wkv, k_pos):
    B, L = q.shape[:2]
    qg = q.reshape(B, L, NSA_GROUPS, NSA_HPG, NSA_HEAD_DIM)
    s = jnp.einsum('blgrd,bkgd->blgrk', qg, wkv[:, :, 0]).astype(F32) * (NSA_HEAD_DIM ** -0.5)
    dist = q_pos[:, None] - k_pos[None, :]
    mask = ((dist >= 0) & (dist < WINDOW) & (k_pos[None, :] >= 0))[None, :, None, None, :]
    p = jax.nn.softmax(jnp.where(mask, s, NEG_INF), axis=-1)
    o = jnp.einsum('blgrk,bkgd->blgrd', p, wkv[:, :, 1])
    return o.reshape(B, L, NSA_HEADS, NSA_HEAD_DIM)


def gate_branches(ng, o_c, o_s, o_w):
    return ng[:, :, 0, :, None] * o_c + ng[:, :, 1, :, None] * o_s + ng[:, :, 2, :, None] * o_w


def nsa_prompt(q, ckv, skv, wkv, ng, cmp_w1, cmp_pe, cmp_w2):
    B, S = q.shape[:2]
    lo, hi = cmp_segment_proj(ckv, cmp_w1)
    ckv_c = cmp_finish(lo, hi, cmp_w1, cmp_pe, cmp_w2)
    n_blocks = S // SEL_BLOCK
    sblk = skv.reshape(B, n_blocks, SEL_BLOCK, 2, NSA_GROUPS, NSA_HEAD_DIM).transpose(0, 4, 1, 2, 3, 5)
    wpad = jnp.pad(wkv, ((0, 0), (WINDOW, 0), (0, 0), (0, 0), (0, 0)))
    nq = S // NSA_QBLK

    def blocks(t):
        return jnp.moveaxis(t.reshape(B, nq, NSA_QBLK, *t.shape[2:]), 1, 0)

    bi = jnp.arange(B)[:, None, None, None]
    gi = jnp.arange(NSA_GROUPS)[None, None, :, None]

    def step(args):
        q_b, g_b, start = args
        q_pos = start + jnp.arange(NSA_QBLK)
        o_c, p_grp = cmp_attend(q_b, q_pos, ckv_c)
        idx, ok = select_blocks(p_grp, q_pos, n_blocks)
        o_s = sel_attend(q_b, q_pos, sblk[bi, gi, idx], idx, ok)
        w_b = lax.dynamic_slice_in_dim(wpad, start, WINDOW + NSA_QBLK, axis=1)
        o_w = win_attend(q_b, q_pos, w_b, start - WINDOW + jnp.arange(WINDOW + NSA_QBLK))
        return gate_branches(g_b, o_c, o_s, o_w)

    o = lax.map(step, (blocks(q), blocks(ng), jnp.arange(nq) * NSA_QBLK))
    return jnp.moveaxis(o, 0, 1).reshape(B, S, NSA_HEADS, NSA_HEAD_DIM)


def nsa_sample(q, ckv, skv, wkv, ng, cache_cmp_kv, cache_sel_kv, cache_win_kv, page_table, l,
               cmp_w1, cmp_pe, cmp_w2):
    DB, L = q.shape[:2]
    n_pages = page_table.shape[1]
    past = n_pages * PAGE_SIZE
    q_pos = past + jnp.arange(L)
    def page_proj(p):
        return cmp_segment_proj(cache_cmp_kv[l, page_table[:, p]], cmp_w1)

    lo_p, hi_p = lax.map(page_proj, jnp.arange(n_pages))
    n_full = (L // CMP_STRIDE) * CMP_STRIDE
    lo_n, hi_n = cmp_segment_proj(ckv[:, :n_full], cmp_w1)

    def flat(t):
        return jnp.moveaxis(t, 0, 1).reshape(DB, -1, *t.shape[3:])

    ckv_c = cmp_finish(jnp.concatenate([flat(lo_p), lo_n], axis=1),
                       jnp.concatenate([flat(hi_p), hi_n], axis=1), cmp_w1, cmp_pe, cmp_w2)
    o_c, p_grp = cmp_attend(q, q_pos, ckv_c)
    n_blocks = -(-(past + L) // SEL_BLOCK)
    past_blocks = past // SEL_BLOCK
    n_new = n_blocks - past_blocks
    bpp = PAGE_SIZE // SEL_BLOCK
    new_blk = jnp.pad(skv, ((0, 0), (0, n_new * SEL_BLOCK - L), (0, 0), (0, 0), (0, 0)))
    new_blk = new_blk.reshape(DB, n_new, SEL_BLOCK, 2, NSA_GROUPS, NSA_HEAD_DIM).transpose(0, 4, 1, 2, 3, 5)
    bi = jnp.arange(DB)[:, None, None]
    gi = jnp.arange(NSA_GROUPS)[None, :, None]

    def sel_token(args):
        q_t, p_t, pos_t = args
        pos1 = pos_t[None]
        idx, ok = select_blocks(p_t[:, None], pos1, n_blocks)
        blk = idx[:, 0]
        jp = jnp.minimum(blk, past_blocks - 1)
        page = page_table[bi, jp // bpp]
        rows = (jp % bpp)[..., None] * SEL_BLOCK + jnp.arange(SEL_BLOCK)
        kb_pool = cache_sel_kv[l, page[..., None], rows, :, gi[..., None], :]
        kb_new = new_blk[bi, gi, jnp.clip(blk - past_blocks, 0, n_new - 1)]
        kb = jnp.where((blk >= past_blocks)[..., None, None, None], kb_new, kb_pool)
        return sel_attend(q_t[:, None], pos1, kb[:, None], idx, ok)[:, 0]

    o_s = jnp.moveaxis(lax.map(sel_token, (jnp.moveaxis(q, 1, 0), jnp.moveaxis(p_grp, 1, 0), q_pos)), 0, 1)
    win_buf = cache_win_kv.shape[2]
    wk = jnp.concatenate([cache_win_kv[l].astype(wkv.dtype), wkv], axis=1)
    o_w = win_attend(q, q_pos, wk, past - win_buf + jnp.arange(win_buf + L))
    return gate_branches(ng, o_c, o_s, o_w), wk[:, L:]


def mixer_output(o_ret, rg, ret_g, o_nsa, mg, w_ret_o, w_nsa_o, w_out):
    B, L = o_ret.shape[:2]
    y_ret = (jax.nn.silu(rg) * rmsnorm(o_ret, ret_g).reshape(B, L, -1)) @ w_ret_o
    y_nsa = o_nsa.reshape(B, L, -1) @ w_nsa_o
    g_ret, g_nsa = jnp.split(jax.nn.sigmoid(mg), 2, axis=-1)
    return (g_ret * y_ret + g_nsa * y_nsa) @ w_out


def grouped_experts(h, expert, gate, w_e1, w_e3, w_e2):
    T, D = h.shape
    tk = T * EXPERT_TOPK
    e_flat = expert.reshape(-1)
    order = jnp.argsort(e_flat)
    e_sorted = e_flat[order]
    tok_sorted = order // EXPERT_TOPK
    gate_sorted = gate.reshape(-1)[order]
    counts = jnp.bincount(e_flat, length=N_EXPERTS)
    padded = (counts + MOE_BLOCK - 1) // MOE_BLOCK * MOE_BLOCK
    pad_end = jnp.cumsum(padded)
    pad_start = pad_end - padded
    cnt_start = jnp.cumsum(counts) - counts
    dest = pad_start[e_sorted] + jnp.arange(tk) - cnt_start[e_sorted]
    n_blk = -(-tk // MOE_BLOCK) + N_EXPERTS
    buf = jnp.zeros((n_blk * MOE_BLOCK, D), h.dtype).at[dest].set(h[tok_sorted])
    blk_expert = jnp.minimum(jnp.searchsorted(pad_end, jnp.arange(n_blk) * MOE_BLOCK, side='right'), N_EXPERTS - 1)

    def run(args):
        xb, e = args
        return (jax.nn.silu(xb @ w_e1[e]) * (xb @ w_e3[e])) @ w_e2[e]

    y_buf = lax.map(run, (buf.reshape(n_blk, MOE_BLOCK, D), blk_expert)).reshape(-1, D)
    return jnp.zeros((T, D), y_buf.dtype).at[tok_sorted].add(y_buf[dest] * gate_sorted[:, None])


def hier_moe(h, w_grp, b_grp, w_exp, b_exp, w_e1, w_e3, w_e2):
    B, L, D = h.shape
    T = B * L
    hf = h.reshape(T, D)
    tok = jnp.arange(T)
    grp_logits = (hf @ w_grp + b_grp).astype(F32)
    grp_prob = jax.nn.softmax(grp_logits, axis=-1)
    g_star = jnp.argmax(grp_logits, axis=-1)
    g_weight = grp_prob[tok, g_star][:, None]
    exp_logits = (hf @ w_exp + b_exp).astype(F32).reshape(T, N_GROUPS, EXPERTS_PER_GROUP)
    top_logit, top_idx = lax.top_k(exp_logits[tok, g_star], EXPERT_TOPK)
    gate = g_weight * jax.nn.softmax(top_logit, axis=-1)
    expert = g_star[:, None] * EXPERTS_PER_GROUP + top_idx
    return grouped_experts(hf, expert, gate, w_e1, w_e3, w_e2).reshape(B, L, D)


def setup_inputs(seed: int = 0) -> dict:
    key = jax.random.key(seed)
    keys = list(jax.random.split(key, 32))

    def nrm(i, shape, scale):
        return jax.random.normal(keys[i], shape, jnp.float32) * scale

    n_pages = PAST_LEN // PAGE_SIZE
    n_phys = (5 * DEC_BATCH * n_pages + 3) // 4
    win_buf = min(WINDOW, PAST_LEN)
    page_table = jax.random.permutation(keys[8], n_phys)[:DEC_BATCH * n_pages].reshape(DEC_BATCH, n_pages).astype(jnp.int32)
    d_ret = RET_HEADS * RET_DV
    d_nsa = NSA_HEADS * NSA_HEAD_DIM
    return {
        'x_prompt': nrm(0, (BATCH, SEQ, D_MODEL), 1.0),
        'x_sample': nrm(1, (DEC_BATCH, DEC_SEQ, D_MODEL), 1.0),
        'c_prompt': nrm(2, (BATCH, D_MODEL), 1.0),
        'c_sample': nrm(3, (DEC_BATCH, D_MODEL), 1.0),
        'state_ret': nrm(4, (DEPTH, DEC_BATCH, RET_HEADS, RET_DK, RET_DV), 0.5),
        'cache_cmp_kv': nrm(5, (DEPTH, n_phys, PAGE_SIZE, 2, NSA_GROUPS, NSA_HEAD_DIM), 1.0),
        'cache_sel_kv': nrm(6, (DEPTH, n_phys, PAGE_SIZE, 2, NSA_GROUPS, NSA_HEAD_DIM), 1.0),
        'cache_win_kv': nrm(7, (DEPTH, DEC_BATCH, win_buf, 2, NSA_GROUPS, NSA_HEAD_DIM), 1.0),
        'page_table': page_table,
        'w_ada': nrm(9, (DEPTH, D_MODEL, 6 * D_MODEL), 0.5 * D_MODEL ** -0.5),
        'b_ada': nrm(10, (DEPTH, 6 * D_MODEL), 0.02),
        'g_norm1': 1.0 + nrm(11, (DEPTH, D_MODEL), 0.02),
        'g_norm2': 1.0 + nrm(12, (DEPTH, D_MODEL), 0.02),
        'g_final': 1.0 + nrm(13, (D_MODEL,), 0.02),
        'w_in': nrm(14, (DEPTH, D_MODEL, D_IN), D_MODEL ** -0.5),
        'ret_norm_g': 1.0 + nrm(15, (DEPTH, RET_DV), 0.02),
        'cmp_w1': nrm(16, (DEPTH, 2, CMP_LEN, NSA_HEAD_DIM, CMP_HID), (CMP_LEN * NSA_HEAD_DIM) ** -0.5),
        'cmp_pe': nrm(17, (DEPTH, 2, CMP_LEN, NSA_HEAD_DIM), 0.5),
        'cmp_w2': nrm(18, (DEPTH, 2, CMP_HID, NSA_HEAD_DIM), CMP_HID ** -0.5),
        'w_ret_o': nrm(19, (DEPTH, d_ret, D_MODEL), d_ret ** -0.5),
        'w_nsa_o': nrm(20, (DEPTH, d_nsa, D_MODEL), d_nsa ** -0.5),
        'w_out': nrm(21, (DEPTH, D_MODEL, D_MODEL), D_MODEL ** -0.5),
        'w_grp': nrm(22, (DEPTH, D_MODEL, N_GROUPS), D_MODEL ** -0.5),
        'b_grp': nrm(23, (DEPTH, N_GROUPS), 0.01),
        'w_exp': nrm(24, (DEPTH, D_MODEL, N_EXPERTS), D_MODEL ** -0.5),
        'b_exp': nrm(25, (DEPTH, N_EXPERTS), 0.01),
        'w_e1': nrm(26, (DEPTH, N_EXPERTS, D_MODEL, D_EXPERT), D_MODEL ** -0.5),
        'w_e3': nrm(27, (DEPTH, N_EXPERTS, D_MODEL, D_EXPERT), D_MODEL ** -0.5),
        'w_e2': nrm(28, (DEPTH, N_EXPERTS, D_EXPERT, D_MODEL), D_EXPERT ** -0.5),
    }


def reference(x_prompt, x_sample, c_prompt, c_sample, state_ret, cache_cmp_kv, cache_sel_kv, cache_win_kv,
              page_table, w_ada, b_ada, g_norm1, g_norm2, g_final, w_in, ret_norm_g, cmp_w1, cmp_pe, cmp_w2,
              w_ret_o, w_nsa_o, w_out, w_grp, b_grp, w_exp, b_exp, w_e1, w_e3, w_e2):
    S = x_prompt.shape[1]
    L = x_sample.shape[1]
    past = page_table.shape[1] * PAGE_SIZE
    win_buf = cache_win_kv.shape[2]
    pos_p = jnp.arange(S)
    pos_s = past + jnp.arange(L)
    xp = x_prompt.astype(F32)
    xs = x_sample.astype(F32)
    ret_p, ret_s, cmp_p, cmp_s, sel_p, sel_s, win_p, win_s = [], [], [], [], [], [], [], []
    for l in range(DEPTH):
        sh1p, sc1p, gt1p, sh2p, sc2p, gt2p = adaln(c_prompt, w_ada[l], b_ada[l])
        sh1s, sc1s, gt1s, sh2s, sc2s, gt2s = adaln(c_sample, w_ada[l], b_ada[l])
        rq, rk, rv, rg, nq, ckv, skv, wkv, ng, mg = mixer_inputs(
            modulate(rmsnorm(xp, g_norm1[l]), sh1p, sc1p), w_in[l], pos_p)
        o_ret, s_fin = retention_prompt(rq, rk, rv)
        o_nsa = nsa_prompt(nq, ckv, skv, wkv, ng, cmp_w1[l], cmp_pe[l], cmp_w2[l])
        xp = xp + gt1p * mixer_output(o_ret, rg, ret_norm_g[l], o_nsa, mg, w_ret_o[l], w_nsa_o[l], w_out[l])
        ret_p.append(s_fin)
        cmp_p.append(ckv)
        sel_p.append(skv)
        win_p.append(jnp.pad(wkv, ((0, 0), (max(win_buf - S, 0), 0), (0, 0), (0, 0), (0, 0)))[:, -win_buf:])
        rq, rk, rv, rg, nq, ckv, skv, wkv, ng, mg = mixer_inputs(
            modulate(rmsnorm(xs, g_norm1[l]), sh1s, sc1s), w_in[l], pos_s)
        s_new, o_ret = retention_chunk(state_ret[l].astype(F32), (rq, rk, rv))
        o_nsa, w_new = nsa_sample(nq, ckv, skv, wkv, ng, cache_cmp_kv, cache_sel_kv, cache_win_kv, page_table, l,
                                  cmp_w1[l], cmp_pe[l], cmp_w2[l])
        xs = xs + gt1s * mixer_output(o_ret, rg, ret_norm_g[l], o_nsa, mg, w_ret_o[l], w_nsa_o[l], w_out[l])
        ret_s.append(s_new)
        cmp_s.append(ckv)
        sel_s.append(skv)
        win_s.append(w_new)
        xp = xp + gt2p * hier_moe(modulate(rmsnorm(xp, g_norm2[l]), sh2p, sc2p),
                                  w_grp[l], b_grp[l], w_exp[l], b_exp[l], w_e1[l], w_e3[l], w_e2[l])
        xs = xs + gt2s * hier_moe(modulate(rmsnorm(xs, g_norm2[l]), sh2s, sc2s),
                                  w_grp[l], b_grp[l], w_exp[l], b_exp[l], w_e1[l], w_e3[l], w_e2[l])
    y_prompt = rmsnorm(xp, g_final).astype(x_prompt.dtype)
    y_sample = rmsnorm(xs, g_final).astype(x_sample.dtype)
    return (y_prompt, y_sample,
            jnp.stack(ret_p).astype(state_ret.dtype), jnp.stack(ret_s).astype(state_ret.dtype),
            jnp.stack(cmp_p).astype(cache_cmp_kv.dtype), jnp.stack(cmp_s).astype(cache_cmp_kv.dtype),
            jnp.stack(sel_p).astype(cache_sel_kv.dtype), jnp.stack(sel_s).astype(cache_sel_kv.dtype),
            jnp.stack(win_p).astype(cache_win_kv.dtype), jnp.stack(win_s).astype(cache_win_kv.dtype))
```

```python
import functools

import numpy as np
import jax
import jax.numpy as jnp
from jax import lax
from jax.experimental import pallas as pl
from jax.experimental.pallas import tpu as pltpu

F32 = jnp.float32
BF16 = jnp.bfloat16
I32 = jnp.int32

D_MODEL = 1024
PAGE_SIZE = 128
RET_HEADS = 8
RET_DK = 64
RET_DV = 128
ROPE_BASE = 10000.0
NSA_HEADS = 16
NSA_GROUPS = 4
NSA_HPG = NSA_HEADS // NSA_GROUPS
NSA_HEAD_DIM = 64
CMP_LEN = 32
CMP_STRIDE = 16
CMP_HID = 128
SEL_BLOCK = 64
SEL_TOPK = 16
WINDOW = 512
N_GROUPS = 4
EXPERTS_PER_GROUP = 8
N_EXPERTS = N_GROUPS * EXPERTS_PER_GROUP
EXPERT_TOPK = 2
D_EXPERT = 512
RMS_EPS = 1e-6
NEG_INF = -1e30
FORCE_SCORE = 1e9

LANES = 128
D_RET_QK = RET_HEADS * RET_DK
D_RET_V = RET_HEADS * RET_DV
D_NSA = NSA_HEADS * NSA_HEAD_DIM
D_KV = 2 * NSA_GROUPS * NSA_HEAD_DIM
D_GK = NSA_GROUPS * NSA_HEAD_DIM
N_GATE = 3 * NSA_HEADS
C_RQ = 0
C_RK = C_RQ + D_RET_QK
C_RV = C_RK + D_RET_QK
C_RG = C_RV + D_RET_V
C_NQ = C_RG + D_RET_V
C_CKV = C_NQ + D_NSA
C_SKV = C_CKV + D_KV
C_WKV = C_SKV + D_KV
C_NG = C_WKV + D_KV
C_MG = C_NG + LANES
C_END = C_MG + 2 * D_MODEL
VMEM_LIMIT = 56 << 20


def _dot(a, b):
    return jnp.dot(a, b, preferred_element_type=F32)


def _dot_nt(a, b):
    return lax.dot_general(a, b, (((1,), (1,)), ((), ())), preferred_element_type=F32)


def _dot_tn(a, b):
    return lax.dot_general(a, b, (((0,), (0,)), ((), ())), preferred_element_type=F32)


def _split(x):
    hi = x.astype(BF16)
    return hi, (x - hi.astype(F32)).astype(BF16)


def _dot3(a, w):
    ah, al = _split(a)
    wh, wl = _split(w)
    return _dot(ah, wh) + _dot(ah, wl) + _dot(al, wh)


def _silu(x):
    return x * jax.nn.sigmoid(x)


def _params(*sem):
    return pltpu.CompilerParams(dimension_semantics=sem, vmem_limit_bytes=VMEM_LIMIT)


def _adaln_kernel(c_ref, w_ref, b_ref, o_ref):
    o_ref[...] = _dot3(_silu(c_ref[...]), w_ref[...]) + b_ref[...]


def _adaln(c, w, b):
    r, d = c.shape
    n = w.shape[1]
    tn = 1024
    return pl.pallas_call(
        _adaln_kernel,
        grid=(n // tn,),
        in_specs=[pl.BlockSpec((r, d), lambda j: (0, 0)),
                  pl.BlockSpec((d, tn), lambda j: (0, j)),
                  pl.BlockSpec((1, tn), lambda j: (0, j))],
        out_specs=pl.BlockSpec((r, tn), lambda j: (0, j)),
        out_shape=jax.ShapeDtypeStruct((r, n), F32),
        compiler_params=_params("arbitrary"),
        name="adaln",
    )(c, w, b.reshape(1, n))


def _rope(x, cos, sin):
    outs = []
    lane = lax.broadcasted_iota(I32, cos.shape, 1)
    first = (lane & 32) == 0
    for j in range(x.shape[1] // LANES):
        xj = x[:, j * LANES:(j + 1) * LANES]
        partner = jnp.where(first, pltpu.roll(xj, LANES - 32, 1), pltpu.roll(xj, 32, 1))
        outs.append(xj * cos + partner * sin)
    return jnp.concatenate(outs, axis=1)


def _inproj_kernel(x_ref, sh_ref, sc_ref, g_ref, cos_ref, sin_ref, w_ref,
                   rq_ref, rk_ref, rv_ref, rg_ref, nq_ref, ckv_ref, skv_ref, wkv_ref,
                   skb_ref, wkb_ref, ng_ref, mg_ref):
    x = x_ref[...]
    h = x * lax.rsqrt(jnp.mean(x * x, axis=-1, keepdims=True) + RMS_EPS) * g_ref[...]
    hb = (h * (1.0 + sc_ref[...]) + sh_ref[...]).astype(BF16)

    def proj(a, b):
        return _dot(hb, w_ref[:, a:b])

    cos = cos_ref[...]
    sin = sin_ref[...]
    rq_ref[...] = _rope(proj(C_RQ, C_RK), cos, sin).astype(rq_ref.dtype)
    rk_ref[...] = (_rope(proj(C_RK, C_RV), cos, sin) * (RET_DK ** -0.5)).astype(rk_ref.dtype)
    rv_ref[...] = proj(C_RV, C_RG).astype(rv_ref.dtype)
    rg_ref[...] = proj(C_RG, C_NQ).astype(rg_ref.dtype)
    nq_ref[...] = (proj(C_NQ, C_CKV) * (NSA_HEAD_DIM ** -0.5)).astype(nq_ref.dtype)
    ckv_ref[...] = proj(C_CKV, C_SKV)
    skv = proj(C_SKV, C_WKV)
    skv_ref[...] = skv
    skb_ref[...] = skv.astype(BF16)
    wkv = proj(C_WKV, C_NG)
    wkv_ref[...] = wkv
    wkb_ref[...] = wkv.astype(BF16)
    ng_ref[...] = jax.nn.sigmoid(proj(C_NG, C_MG))
    mg_ref[...] = jax.nn.sigmoid(proj(C_MG, C_END)).astype(BF16)


def _inproj(x, shift, scale, g, cos, sin, w_pad, *, tm, rows_per_mod, pos_blocks, act_dtype):
    t, d = x.shape
    r = shift.shape[1]
    mod_spec = pl.BlockSpec((None, r, d), lambda i: (i // rows_per_mod, 0, 0))
    pos_spec = pl.BlockSpec((tm, LANES), lambda i: (i % pos_blocks, 0))

    def out(cols, dt):
        return pl.BlockSpec((tm, cols), lambda i: (i, 0)), jax.ShapeDtypeStruct((t, cols), dt)

    outs = [out(D_RET_QK, act_dtype), out(D_RET_QK, act_dtype), out(D_RET_V, act_dtype), out(D_RET_V, act_dtype),
            out(D_NSA, act_dtype), out(D_KV, F32), out(D_KV, F32), out(D_KV, F32),
            out(D_KV, BF16), out(D_KV, BF16), out(LANES, F32), out(2 * D_MODEL, BF16)]
    return pl.pallas_call(
        _inproj_kernel,
        grid=(t // tm,),
        in_specs=[pl.BlockSpec((tm, d), lambda i: (i, 0)), mod_spec, mod_spec,
                  pl.BlockSpec((1, d), lambda i: (0, 0)), pos_spec, pos_spec,
                  pl.BlockSpec((d, C_END), lambda i: (0, 0), pipeline_mode=pl.Buffered(1))],
        out_specs=[o[0] for o in outs],
        out_shape=[o[1] for o in outs],
        compiler_params=_params("arbitrary"),
        name="inproj",
    )(x, shift, scale, g.reshape(1, d), cos, sin, w_pad)


def _rope_tables(pos):
    half = RET_DK // 2
    inv_freq = ROPE_BASE ** (-jnp.arange(half, dtype=F32) / half)
    ang = pos.astype(F32)[:, None] * inv_freq[None, :]
    cos, sin = jnp.cos(ang), jnp.sin(ang)
    return jnp.tile(cos, (1, 4)), jnp.tile(jnp.concatenate([-sin, sin], axis=1), (1, 2))


def _pad_w_in(w_in):
    d = w_in.shape[0]
    n_real = C_NG + N_GATE
    return jnp.concatenate([w_in[:, :n_real], jnp.zeros((d, LANES - N_GATE), w_in.dtype), w_in[:, n_real:]],
                           axis=1).astype(BF16)


def _ret_log_decay():
    return np.log1p(-np.exp2(-5.0 - np.arange(RET_HEADS, dtype=np.float32))).astype(np.float32)


def _ret_head_out(o, g, gn):
    n = o * lax.rsqrt(jnp.mean(o * o, axis=-1, keepdims=True) + RMS_EPS) * gn
    return _silu(g) * n


def _ret_prompt_kernel(q_ref, k_ref, v_ref, g_ref, dm_ref, qd_ref, kd_ref, gn_ref, y_ref, sf_ref, s_scr, *, sdec):
    j = pl.program_id(1)

    @pl.when(j == 0)
    def _():
        s_scr[...] = jnp.zeros_like(s_scr)

    qb = q_ref[...]
    kb = k_ref[...]
    qd = (qb.astype(F32) * qd_ref[...]).astype(BF16)
    kd = (kb.astype(F32) * kd_ref[...]).astype(BF16)
    gn = gn_ref[...]
    for h in range(RET_HEADS):
        sl = slice(h * RET_DK, (h + 1) * RET_DK)
        vl = slice(h * RET_DV, (h + 1) * RET_DV)
        vh = v_ref[:, vl]
        s_h = s_scr[h]
        sc = _dot_nt(qb[:, sl], kb[:, sl]) * dm_ref[h]
        o = _dot(sc.astype(BF16), vh) + _dot(qd[:, sl], s_h.astype(BF16))
        s_scr[h] = sdec[h] * s_h + _dot_tn(kd[:, sl], vh)
        y_ref[:, vl] = _ret_head_out(o, g_ref[:, vl].astype(F32), gn).astype(y_ref.dtype)

    @pl.when(j == pl.num_programs(1) - 1)
    def _():
        sf_ref[...] = s_scr[...]


def _ret_tables(c):
    lg = _ret_log_decay()
    i = np.arange(c, dtype=np.float32)
    diff = i[:, None] - i[None, :]
    dm = np.where(diff[None] >= 0, np.exp(lg[:, None, None] * np.maximum(diff, 0.0)[None]), 0.0).astype(np.float32)
    qd = np.repeat(np.exp(lg[None, :] * (i[:, None] + 1.0)), RET_DK, axis=1).astype(np.float32)
    kd = np.repeat(np.exp(lg[None, :] * (c - 1.0 - i)[:, None]), RET_DK, axis=1).astype(np.float32)
    sdec = tuple(float(v) for v in np.exp(lg * np.float32(c)))
    return dm, qd, kd, sdec


def _ret_prompt(rq, rk, rv, rg, gn, *, batch, seq):
    c = 128
    n = seq // c
    dm, qd, kd, sdec = _ret_tables(c)
    row = lambda b, j: (b * n + j, 0)
    const2 = lambda b, j: (0, 0)
    return pl.pallas_call(
        functools.partial(_ret_prompt_kernel, sdec=sdec),
        grid=(batch, n),
        in_specs=[pl.BlockSpec((c, D_RET_QK), row), pl.BlockSpec((c, D_RET_QK), row),
                  pl.BlockSpec((c, D_RET_V), row), pl.BlockSpec((c, D_RET_V), row),
                  pl.BlockSpec((RET_HEADS, c, c), lambda b, j: (0, 0, 0)),
                  pl.BlockSpec((c, D_RET_QK), const2), pl.BlockSpec((c, D_RET_QK), const2),
                  pl.BlockSpec((1, RET_DV), const2)],
        out_specs=[pl.BlockSpec((c, D_RET_V), row),
                   pl.BlockSpec((None, RET_HEADS, RET_DK, RET_DV), lambda b, j: (b, 0, 0, 0))],
        out_shape=[jax.ShapeDtypeStruct((batch * seq, D_RET_V), BF16),
                   jax.ShapeDtypeStruct((batch, RET_HEADS, RET_DK, RET_DV), F32)],
        scratch_shapes=[pltpu.VMEM((RET_HEADS, RET_DK, RET_DV), F32)],
        compiler_params=_params("arbitrary", "arbitrary"),
        name="ret_prompt",
    )(rq, rk, rv, rg, jnp.asarray(dm), jnp.asarray(qd), jnp.asarray(kd), gn.reshape(1, RET_DV))


def _ret_sample_kernel(q_ref, k_ref, v_ref, g_ref, s0_ref, dm_ref, qd_ref, kd_ref, gn_ref, y_ref, sn_ref,
                       *, sdec, n_req, n_tok):
    rows = n_req * n_tok
    q = q_ref[...]
    k = k_ref[...]
    qd = q * qd_ref[...]
    kd = k * kd_ref[...]
    qb = q.astype(BF16)
    kb = k.astype(BF16)
    gn = gn_ref[...]
    req = lax.broadcasted_iota(I32, (rows, 1), 0) // n_tok
    for h in range(RET_HEADS):
        sl = slice(h * RET_DK, (h + 1) * RET_DK)
        vl = slice(h * RET_DV, (h + 1) * RET_DV)
        vh = v_ref[:, vl].astype(BF16)
        sc = _dot_nt(qb[:, sl], kb[:, sl]) * dm_ref[h]
        o = _dot(sc.astype(BF16), vh)
        qd_h = qd[:, sl]
        kd_h = kd[:, sl]
        for r in range(n_req):
            own = req == r
            s_r = s0_ref[r, h]
            o = o + _dot(jnp.where(own, qd_h, 0.0).astype(BF16), s_r.astype(BF16))
            sn_ref[r, h] = sdec[h] * s_r + _dot_tn(jnp.where(own, kd_h, 0.0).astype(BF16), vh)
        y_ref[:, vl] = _ret_head_out(o, g_ref[:, vl], gn).astype(y_ref.dtype)


def _ret_sample(rq, rk, rv, rg, state, gn, *, n_tok):
    db = state.shape[0]
    n_req = 16
    rows = n_req * n_tok
    lg = _ret_log_decay()
    i = np.arange(rows)
    t = (i % n_tok).astype(np.float32)
    same = (i[:, None] // n_tok) == (i[None, :] // n_tok)
    diff = t[:, None] - t[None, :]
    dm = np.where((same & (diff >= 0))[None], np.exp(lg[:, None, None] * np.maximum(diff, 0.0)[None]), 0.0)
    qd = np.repeat(np.exp(lg[None, :] * (t[:, None] + 1.0)), RET_DK, axis=1)
    kd = np.repeat(np.exp(lg[None, :] * (n_tok - 1.0 - t)[:, None]), RET_DK, axis=1)
    sdec = tuple(float(v) for v in np.exp(lg * np.float32(n_tok)))
    row = lambda i: (i, 0)
    const2 = lambda i: (0, 0)
    st_spec = pl.BlockSpec((n_req, RET_HEADS, RET_DK, RET_DV), lambda i: (i, 0, 0, 0))
    return pl.pallas_call(
        functools.partial(_ret_sample_kernel, sdec=sdec, n_req=n_req, n_tok=n_tok),
        grid=(db // n_req,),
        in_specs=[pl.BlockSpec((rows, D_RET_QK), row), pl.BlockSpec((rows, D_RET_QK), row),
                  pl.BlockSpec((rows, D_RET_V), row), pl.BlockSpec((rows, D_RET_V), row), st_spec,
                  pl.BlockSpec((RET_HEADS, rows, rows), lambda i: (0, 0, 0)),
                  pl.BlockSpec((rows, D_RET_QK), const2), pl.BlockSpec((rows, D_RET_QK), const2),
                  pl.BlockSpec((1, RET_DV), const2)],
        out_specs=[pl.BlockSpec((rows, D_RET_V), row), st_spec],
        out_shape=[jax.ShapeDtypeStruct((db * n_tok, D_RET_V), BF16),
                   jax.ShapeDtypeStruct(state.shape, F32)],
        compiler_params=_params("arbitrary"),
        name="ret_sample",
    )(rq, rk, rv, rg, state, jnp.asarray(dm, F32), jnp.asarray(qd, F32), jnp.asarray(kd, F32),
      gn.reshape(1, RET_DV))


def _cmp_weights(cmp_w1, cmp_pe, cmp_w2):
    w1c = jnp.concatenate([cmp_w1[:, :CMP_STRIDE], cmp_w1[:, CMP_STRIDE:]], axis=-1).astype(BF16)
    eye = jnp.eye(NSA_GROUPS, dtype=cmp_w2.dtype)
    w2bd = jnp.einsum('gk,chd->cghkd', eye, cmp_w2).reshape(2, NSA_GROUPS * CMP_HID, D_GK).astype(BF16)
    pe8 = jnp.broadcast_to(cmp_pe.reshape(2, 1, CMP_LEN * NSA_HEAD_DIM), (2, 8, CMP_LEN * NSA_HEAD_DIM))
    w1f = cmp_w1.reshape(2, CMP_LEN * NSA_HEAD_DIM, CMP_HID)
    r = np.arange(PAGE_SIZE)
    seg_pp = PAGE_SIZE // CMP_STRIDE
    perm = r[None, :] == (r[:, None] % seg_pp) * CMP_STRIDE + r[:, None] // seg_pp
    return jnp.asarray(perm, BF16), w1c, w2bd, pe8, w1f


def _cmp_pe_term(pe8_ref, w1f_ref, c):
    return _dot3(pe8_ref[c], w1f_ref[c])[0:1]


def _cmp_partial(pages, perm, w1c_ref, c):
    seg_pp = PAGE_SIZE // CMP_STRIDE
    ys = [_dot(perm, pg[:, c * D_GK:(c + 1) * D_GK].astype(BF16)) for pg in pages]
    acc = None
    for p in range(CMP_STRIDE):
        xp = jnp.concatenate([y[p * seg_pp:(p + 1) * seg_pp] for y in ys], axis=0)
        xs = jnp.concatenate([xp[:, g * NSA_HEAD_DIM:(g + 1) * NSA_HEAD_DIM] for g in range(NSA_GROUPS)], axis=0)
        d = _dot(xs.astype(BF16), w1c_ref[c, p])
        acc = d if acc is None else acc + d
    return acc


def _cmp_store(pages, perm, w1c_ref, lohi_scr, step):
    seg_ps = len(pages) * (PAGE_SIZE // CMP_STRIDE)
    off = pl.multiple_of(step * seg_ps, seg_ps)
    for c in range(2):
        acc = _cmp_partial(pages, perm, w1c_ref, c)
        for g in range(NSA_GROUPS):
            lohi_scr[c, g, pl.ds(off, seg_ps), :] = acc[g * seg_ps:(g + 1) * seg_ps]


def _cmp_finish(lohi, pe, w2bd_c):
    hs = []
    for g in range(NSA_GROUPS):
        lh = lohi(g)
        nseg = lh.shape[0]
        z = lh[:, :CMP_HID] + pltpu.roll(lh[:, CMP_HID:], nseg - 1, 0) + pe
        hs.append(_silu(z).astype(BF16))
    return _dot(jnp.concatenate(hs, axis=1), w2bd_c)


def _q_blockdiag(qf, tq):
    lane = lax.broadcasted_iota(I32, (tq, D_GK), 1)
    blocks = []
    for g in range(NSA_GROUPS):
        slab = qf[:, g * D_GK:(g + 1) * D_GK]
        keep = (lane >= g * NSA_HEAD_DIM) & (lane < (g + 1) * NSA_HEAD_DIM)
        for r in range(NSA_HPG):
            sh = (NSA_HEAD_DIM * (g - r)) % D_GK
            rolled = slab if sh == 0 else pltpu.roll(slab, sh, 1)
            blocks.append(jnp.where(keep, rolled, 0.0))
    return jnp.concatenate(blocks, axis=0).astype(BF16)


def _pair_pv(p, v):
    half = p.shape[0] // 2
    return jnp.concatenate([_dot(p[:half], v[:, :LANES]), _dot(p[half:], v[:, LANES:])], axis=0)


def _assemble_heads(acc, tq):
    lane = lax.broadcasted_iota(I32, (tq, LANES), 1)
    chunks = []
    for c in range(NSA_HEADS // 2):
        a = acc[(2 * c) * tq:(2 * c + 1) * tq]
        b = acc[(2 * c + 1) * tq:(2 * c + 2) * tq]
        if (c // 2) % 2 == 0:
            b = pltpu.roll(b, NSA_HEAD_DIM, 1)
        else:
            a = pltpu.roll(a, NSA_HEAD_DIM, 1)
        chunks.append(jnp.where(lane < NSA_HEAD_DIM, a, b))
    return jnp.concatenate(chunks, axis=1)


def _gate_branches(ng, eg_ref, branches, tq):
    nh, nl = _split(ng)
    out = None
    for br, acc in enumerate(branches):
        gexp = _dot(nh, eg_ref[br]) + _dot(nl, eg_ref[br])
        term = gexp * _assemble_heads(acc, tq)
        out = term if out is None else out + term
    return out


def _masked_softmax(s, mask):
    s = jnp.where(mask, s, NEG_INF)
    m = jnp.max(s, axis=-1, keepdims=True)
    e = jnp.exp(s - m)
    return e / jnp.sum(e, axis=-1, keepdims=True)


def _rank_select(sc, sc_scr, valid, n_rows):
    sc_scr[...] = sc
    n_io = lax.broadcasted_iota(I32, sc.shape, 0)

    def body(m, cnt):
        row = sc_scr[pl.ds(m, 1), :]
        beats = (row > sc) | ((row == sc) & (m < n_io))
        return cnt + jnp.where(beats, 1, 0)

    cnt = lax.fori_loop(0, n_rows, body, jnp.zeros(sc.shape, I32))
    return jnp.where((cnt < SEL_TOPK) & valid, 1.0, 0.0)


def _block_scores(scT, pos, n_io):
    valid = n_io * SEL_BLOCK <= pos
    cur = pos // SEL_BLOCK
    forced = (n_io == 0) | (n_io == cur) | (n_io == cur - 1)
    return jnp.where(forced, FORCE_SCORE, jnp.where(valid, scT, -FORCE_SCORE)), valid


def _nsa_tables(n_cmp_pad, n_cmp, n_blocks, nb_pad, n_keys_pad, tk):
    c = np.arange(n_cmp_pad)[None, :]
    n = np.arange(nb_pad)[:, None]
    ov = np.clip(np.minimum(CMP_STRIDE * c + CMP_LEN, SEL_BLOCK * n + SEL_BLOCK)
                 - np.maximum(CMP_STRIDE * c, SEL_BLOCK * n), 0, None) / CMP_STRIDE
    ov = np.where((c < n_cmp) & (n < n_blocks), ov, 0.0)
    key = np.arange(n_keys_pad)
    e = (key[None, :] // SEL_BLOCK == np.arange(LANES * ((nb_pad + LANES - 1) // LANES))[:, None])
    e3 = e.reshape(e.shape[0], n_keys_pad // tk, tk).transpose(1, 0, 2)
    col = np.arange(LANES)[:, None]
    lane = np.arange(D_NSA)[None, :]
    eg = np.stack([col == br * NSA_HEADS + lane // NSA_HEAD_DIM for br in range(3)])
    return jnp.asarray(ov, BF16), jnp.asarray(e3, BF16), jnp.asarray(eg, BF16)


def _cmp_prompt_kernel(ck_ref, perm_ref, w1c_ref, w2bd_ref, pe8_ref, w1f_ref, ko_ref, vo_ref, lohi_scr):
    st = pl.program_id(1)
    n_pg = ck_ref.shape[0] // PAGE_SIZE
    pages = [ck_ref[k * PAGE_SIZE:(k + 1) * PAGE_SIZE, :] for k in range(n_pg)]
    _cmp_store(pages, perm_ref[...], w1c_ref, lohi_scr, st)

    @pl.when(st == pl.num_programs(1) - 1)
    def _():
        for c, o_ref in enumerate((ko_ref, vo_ref)):
            pe = _cmp_pe_term(pe8_ref, w1f_ref, c)
            o_ref[...] = _cmp_finish(lambda g: lohi_scr[c, g], pe, w2bd_ref[c]).astype(o_ref.dtype)


def _cmp_prompt(ckv, perm, w1c, w2bd, pe8, w1f, *, batch, seq):
    nseg = seq // CMP_STRIDE
    rows = 8 * PAGE_SIZE
    n_st = seq // rows
    full = lambda a: pl.BlockSpec(a.shape, lambda b, s: (0,) * a.ndim)
    o_spec = pl.BlockSpec((nseg, D_GK), lambda b, s: (b, 0))
    o_shape = jax.ShapeDtypeStruct((batch * nseg, D_GK), BF16)
    return pl.pallas_call(
        _cmp_prompt_kernel,
        grid=(batch, n_st),
        in_specs=[pl.BlockSpec((rows, D_KV), lambda b, s: (b * n_st + s, 0)),
                  full(perm), full(w1c), full(w2bd), full(pe8), full(w1f)],
        out_specs=[o_spec, o_spec],
        out_shape=[o_shape, o_shape],
        scratch_shapes=[pltpu.VMEM((2, NSA_GROUPS, nseg, 2 * CMP_HID), F32)],
        compiler_params=_params("arbitrary", "arbitrary"),
        name="cmp_prompt",
    )(ckv, perm, w1c, w2bd, pe8, w1f)


def _nsa_prompt_kernel(q_ref, ng_ref, ck_ref, cv_ref, sk_ref, sv_ref, wk_ref, wv_ref, ov_ref, e_ref, eg_ref,
                       o_ref, acc_scr, m_scr, l_scr, sc_scr, *, tq, tk, wlen):
    seq = sk_ref.shape[0]
    n_cmp_pad = ck_ref.shape[0]
    nb = ov_ref.shape[0]
    rows = NSA_HEADS * tq
    grows = NSA_GROUPS * tq
    q0 = pl.program_id(1) * tq
    qbd = _q_blockdiag(q_ref[...].astype(F32), tq)
    pos_r = q0 + lax.broadcasted_iota(I32, (rows, 1), 0) % tq

    s = _dot_nt(qbd, ck_ref[...])
    cidx = lax.broadcasted_iota(I32, (rows, n_cmp_pad), 1)
    cmask = (CMP_STRIDE * cidx + (CMP_LEN - 1) <= pos_r) & (cidx < n_cmp_pad - 1)
    p = jnp.where(cmask, _masked_softmax(s, cmask), 0.0)
    o_cmp = _pair_pv(p.astype(BF16), cv_ref[...])
    pg = jnp.sum(p.reshape(NSA_GROUPS, NSA_HPG, tq, n_cmp_pad), axis=1).reshape(grows, n_cmp_pad)
    ph, plo = _split(pg)
    sc_t = _dot_nt(ov_ref[...], ph) + _dot_nt(ov_ref[...], plo)

    n_io = lax.broadcasted_iota(I32, (nb, grows), 0)
    pos_l = q0 + lax.broadcasted_iota(I32, (nb, grows), 1) % tq
    sc, valid = _block_scores(sc_t, pos_l, n_io)
    sel_t = _rank_select(sc, sc_scr, valid, nb)
    sel = jnp.concatenate([sel_t, jnp.zeros((LANES - nb, grows), F32)], axis=0).T.astype(BF16)

    m_scr[...] = jnp.full(m_scr.shape, NEG_INF, F32)
    l_scr[...] = jnp.zeros(l_scr.shape, F32)
    acc_scr[...] = jnp.zeros(acc_scr.shape, F32)
    pos_g = q0 + lax.broadcasted_iota(I32, (grows, tk), 0) % tq
    lane_k = lax.broadcasted_iota(I32, (grows, tk), 1)

    def kv_step(j, carry):
        k0 = pl.multiple_of(j * tk, tk)
        st = _dot_nt(qbd, sk_ref[pl.ds(k0, tk), :])
        selm = _dot(sel, e_ref[j])
        bias = jnp.where((selm > 0.5) & (k0 + lane_k <= pos_g), 0.0, NEG_INF)
        st = (st.reshape(NSA_GROUPS, NSA_HPG, tq, tk) + bias.reshape(NSA_GROUPS, 1, tq, tk)).reshape(rows, tk)
        m_prev = m_scr[...]
        m_new = jnp.maximum(m_prev, jnp.max(st, axis=-1, keepdims=True))
        alpha = jnp.exp(m_prev - m_new)
        pt = jnp.exp(st - m_new)
        l_scr[...] = alpha * l_scr[...] + jnp.sum(pt, axis=-1, keepdims=True)
        acc_scr[...] = alpha * acc_scr[...] + _pair_pv(pt.astype(BF16), sv_ref[pl.ds(k0, tk), :])
        m_scr[...] = m_new
        return carry

    lax.fori_loop(0, (q0 + tq + tk - 1) // tk, kv_step, 0)
    o_sel = acc_scr[...] / l_scr[...]

    w0 = pl.multiple_of(jnp.clip(q0 + tq - wlen, 0, seq - wlen), SEL_BLOCK)
    sw = _dot_nt(qbd, wk_ref[pl.ds(w0, wlen), :])
    dist = pos_r - (w0 + lax.broadcasted_iota(I32, (rows, wlen), 1))
    pw = _masked_softmax(sw, (dist >= 0) & (dist < WINDOW))
    o_win = _pair_pv(pw.astype(BF16), wv_ref[pl.ds(w0, wlen), :])

    o_ref[...] = _gate_branches(ng_ref[...], eg_ref, (o_cmp, o_sel, o_win), tq).astype(o_ref.dtype)


def _nsa_prompt(nq, ng, ck, cv, skb, wkb, *, batch, seq):
    tq, tk, wlen = 64, 256, 640
    assert seq % tk == 0 and seq >= wlen and tq == SEL_BLOCK
    nseg = seq // CMP_STRIDE
    n_blocks = seq // SEL_BLOCK
    nq_t = seq // tq
    ov, e3, eg = _nsa_tables(nseg, nseg - 1, n_blocks, n_blocks, seq, tk)
    rows = NSA_HEADS * tq
    qrow = lambda b, i: (b * nq_t + i, 0)
    full = lambda a: pl.BlockSpec(a.shape, lambda b, i: (0,) * a.ndim)
    kspec = pl.BlockSpec((seq, D_GK), lambda b, i: (b, 0))
    vspec = pl.BlockSpec((seq, D_GK), lambda b, i: (b, 1))
    return pl.pallas_call(
        functools.partial(_nsa_prompt_kernel, tq=tq, tk=tk, wlen=wlen),
        grid=(batch, nq_t),
        in_specs=[pl.BlockSpec((tq, D_NSA), qrow), pl.BlockSpec((tq, LANES), qrow),
                  pl.BlockSpec((nseg, D_GK), lambda b, i: (b, 0)), pl.BlockSpec((nseg, D_GK), lambda b, i: (b, 0)),
                  kspec, vspec, kspec, vspec, full(ov), full(e3), full(eg)],
        out_specs=pl.BlockSpec((tq, D_NSA), qrow),
        out_shape=jax.ShapeDtypeStruct((batch * seq, D_NSA), BF16),
        scratch_shapes=[pltpu.VMEM((rows, LANES), F32), pltpu.VMEM((rows, 1), F32), pltpu.VMEM((rows, 1), F32),
                        pltpu.VMEM((n_blocks, NSA_GROUPS * tq), F32)],
        compiler_params=_params("arbitrary", "arbitrary"),
        name="nsa_prompt",
    )(nq, ng, ck, cv, skb, skb, wkb, wkb, ov, e3, eg)


def _nsa_sample_kernel(pt_ref, *refs, n_tok, ppb, tk, n_blocks, past):
    cmp_refs = refs[:ppb]
    sel_refs = refs[ppb:2 * ppb]
    (q_ref, ng_ref, snew_ref, wnew_ref, win_ref, perm_ref, w1c_ref, w2bd_ref, pe8_ref, w1f_ref, ov_ref, e_ref,
     eg_ref, o_ref, wout_ref, lohi_scr, sel_scr, win_scr, pe_scr, sc_scr) = refs[2 * ppb:]
    del pt_ref
    b = pl.program_id(0)
    st = pl.program_id(1)
    n_st = pl.num_programs(1)
    nseg = lohi_scr.shape[2]
    tq = n_tok
    rows = NSA_HEADS * tq
    n_keys_pad = sel_scr.shape[0]

    @pl.when((b == 0) & (st == 0))
    def _():
        for c in range(2):
            pe_scr[c] = jnp.broadcast_to(_cmp_pe_term(pe8_ref, w1f_ref, c), (8, CMP_HID))

    _cmp_store([r[...] for r in cmp_refs], perm_ref[...], w1c_ref, lohi_scr, st)
    for k, r in enumerate(sel_refs):
        off = pl.multiple_of(st * (ppb * PAGE_SIZE) + k * PAGE_SIZE, PAGE_SIZE)
        sel_scr[pl.ds(off, PAGE_SIZE), :] = r[...].astype(BF16)

    @pl.when(st == n_st - 1)
    def _():
        qbd = _q_blockdiag(q_ref[...], tq)
        pos_r = past + lax.broadcasted_iota(I32, (rows, 1), 0) % tq

        kc = _cmp_finish(lambda g: lohi_scr[0, g], pe_scr[0, 0:1], w2bd_ref[0]).astype(BF16)
        vc = _cmp_finish(lambda g: lohi_scr[1, g], pe_scr[1, 0:1], w2bd_ref[1]).astype(BF16)
        s = _dot_nt(qbd, kc)
        cidx = lax.broadcasted_iota(I32, (rows, nseg), 1)
        cmask = (CMP_STRIDE * cidx + (CMP_LEN - 1) <= pos_r) & (cidx < nseg - 1)
        p = jnp.where(cmask, _masked_softmax(s, cmask), 0.0)
        o_cmp = _pair_pv(p.astype(BF16), vc)
        pg = jnp.sum(p.reshape(NSA_GROUPS, NSA_HPG, tq, nseg), axis=1, keepdims=True)
        pg = jnp.broadcast_to(pg, (NSA_GROUPS, NSA_HPG, tq, nseg)).reshape(rows, nseg)
        ph, plo = _split(pg)
        sc_t = _dot_nt(ov_ref[...], ph) + _dot_nt(ov_ref[...], plo)

        nb_pad = ov_ref.shape[0]
        n_io = lax.broadcasted_iota(I32, (nb_pad, rows), 0)
        pos_l = past + lax.broadcasted_iota(I32, (nb_pad, rows), 1) % tq
        sc, valid = _block_scores(sc_t, pos_l, n_io)
        sel = _rank_select(sc, sc_scr, valid & (n_io < n_blocks), n_blocks).T.astype(BF16)

        new_blk = jnp.concatenate([snew_ref[...], jnp.zeros((n_keys_pad - past - tq, D_KV), F32)], axis=0)
        sel_scr[pl.ds(past, n_keys_pad - past), :] = new_blk.astype(BF16)
        lane_k = lax.broadcasted_iota(I32, (rows, tk), 1)

        def kv_step(j, carry):
            m_prev, l_prev, acc = carry
            k0 = pl.multiple_of(j * tk, tk)
            kv = sel_scr[pl.ds(k0, tk), :]
            stile = _dot_nt(qbd, kv[:, :D_GK])
            selm = _dot(sel, e_ref[j])
            stile = jnp.where((selm > 0.5) & (k0 + lane_k <= pos_r), stile, NEG_INF)
            m_new = jnp.maximum(m_prev, jnp.max(stile, axis=-1, keepdims=True))
            alpha = jnp.exp(m_prev - m_new)
            pt = jnp.exp(stile - m_new)
            l_new = alpha * l_prev + jnp.sum(pt, axis=-1, keepdims=True)
            return m_new, l_new, alpha * acc + _pair_pv(pt.astype(BF16), kv[:, D_GK:])

        init = (jnp.full((rows, 1), NEG_INF, F32), jnp.zeros((rows, 1), F32), jnp.zeros((rows, LANES), F32))
        _, l_fin, acc = lax.fori_loop(0, n_keys_pad // tk, kv_step, init)
        o_sel = acc / l_fin

        win_buf = win_ref.shape[0]
        wlen = win_scr.shape[0]
        win_scr[0:win_buf, :] = win_ref[...]
        win_scr[win_buf:wlen, :] = jnp.concatenate(
            [wnew_ref[...], jnp.zeros((wlen - win_buf - tq, D_KV), F32)], axis=0)
        wkv = win_scr[...].astype(BF16)
        sw = _dot_nt(qbd, wkv[:, :D_GK])
        k_pos = past - win_buf + lax.broadcasted_iota(I32, (rows, wlen), 1)
        dist = pos_r - k_pos
        pw = _masked_softmax(sw, (dist >= 0) & (dist < WINDOW) & (k_pos >= 0))
        o_win = _pair_pv(pw.astype(BF16), wkv[:, D_GK:])

        o_ref[...] = _gate_branches(ng_ref[...], eg_ref, (o_cmp, o_sel, o_win), tq)
        wout_ref[0:win_buf - tq, :] = win_ref[tq:win_buf, :]
        wout_ref[win_buf - tq:win_buf, :] = wnew_ref[...]


def _nsa_sample(nq, ng, skv, wkv, cache_cmp, cache_sel, cache_win, page_table, perm, w1c, w2bd, pe8, w1f, *, n_tok):
    db, n_pages = page_table.shape
    past = n_pages * PAGE_SIZE
    win_buf = cache_win.shape[1]
    ppb, tk, wlen = 8, 256, 640
    assert n_tok == 8 and n_tok < CMP_STRIDE and n_pages % ppb == 0 and win_buf + n_tok <= wlen and win_buf >= n_tok
    nseg = past // CMP_STRIDE
    n_blocks = -(-(past + n_tok) // SEL_BLOCK)
    nb_pad = -(-n_blocks // LANES) * LANES
    n_keys_pad = -(-(n_blocks * SEL_BLOCK) // tk) * tk
    ov, e3, eg = _nsa_tables(nseg, nseg - 1, n_blocks, nb_pad, n_keys_pad, tk)
    rows = NSA_HEADS * n_tok

    def page_spec(k):
        return pl.BlockSpec((None, PAGE_SIZE, D_KV), lambda b, s, pt: (pt[b, s * ppb + k], 0, 0))

    req = lambda cols: pl.BlockSpec((n_tok, cols), lambda b, s, pt: (b, 0))
    full = lambda a: pl.BlockSpec(a.shape, lambda b, s, pt: (0,) * a.ndim)
    grid_spec = pltpu.PrefetchScalarGridSpec(
        num_scalar_prefetch=1,
        grid=(db, n_pages // ppb),
        in_specs=[page_spec(k) for k in range(ppb)] + [page_spec(k) for k in range(ppb)]
        + [req(D_NSA), req(LANES), req(D_KV), req(D_KV),
           pl.BlockSpec((None, win_buf, D_KV), lambda b, s, pt: (b, 0, 0)),
           full(perm), full(w1c), full(w2bd), full(pe8), full(w1f), full(ov), full(e3), full(eg)],
        out_specs=[req(D_NSA), pl.BlockSpec((None, win_buf, D_KV), lambda b, s, pt: (b, 0, 0))],
        scratch_shapes=[pltpu.VMEM((2, NSA_GROUPS, nseg, 2 * CMP_HID), F32),
                        pltpu.VMEM((n_keys_pad, D_KV), BF16),
                        pltpu.VMEM((wlen, D_KV), F32),
                        pltpu.VMEM((2, 8, CMP_HID), F32),
                        pltpu.VMEM((nb_pad, rows), F32)],
    )
    return pl.pallas_call(
        functools.partial(_nsa_sample_kernel, n_tok=n_tok, ppb=ppb, tk=tk, n_blocks=n_blocks, past=past),
        grid_spec=grid_spec,
        out_shape=[jax.ShapeDtypeStruct((db * n_tok, D_NSA), F32),
                   jax.ShapeDtypeStruct((db, win_buf, D_KV), F32)],
        compiler_params=_params("arbitrary", "arbitrary"),
        name="nsa_sample",
    )(page_table, *([cache_cmp] * ppb), *([cache_sel] * ppb), nq, ng, skv, wkv, cache_win,
      perm, w1c, w2bd, pe8, w1f, ov, e3, eg)


def _route(lg):
    lane = lax.broadcasted_iota(I32, lg.shape, 1).astype(F32)
    big = float(1 << 20)
    is_g = lane < N_GROUPS
    gl = jnp.where(is_g, lg, -jnp.inf)
    gmax = jnp.max(gl, axis=-1, keepdims=True)
    g_star = jnp.min(jnp.where(gl == gmax, lane, big), axis=-1, keepdims=True)
    g_w = 1.0 / jnp.sum(jnp.where(is_g, jnp.exp(lg - gmax), 0.0), axis=-1, keepdims=True)
    lo = N_GROUPS + g_star * EXPERTS_PER_GROUP
    el = jnp.where((lane >= lo) & (lane < lo + EXPERTS_PER_GROUP), lg, -jnp.inf)
    m1 = jnp.max(el, axis=-1, keepdims=True)
    i1 = jnp.min(jnp.where(el == m1, lane, big), axis=-1, keepdims=True)
    el2 = jnp.where(lane == i1, -jnp.inf, el)
    m2 = jnp.max(el2, axis=-1, keepdims=True)
    i2 = jnp.min(jnp.where(el2 == m2, lane, big), axis=-1, keepdims=True)
    e2 = jnp.exp(m2 - m1)
    p1 = 1.0 / (1.0 + e2)
    out = jnp.where(lane == 0, i1 - N_GROUPS, 0.0)
    out = jnp.where(lane == 1, i2 - N_GROUPS, out)
    out = jnp.where(lane == 2, g_w * p1, out)
    return jnp.where(lane == 3, g_w * (e2 * p1), out)


def _mixout_kernel(x_ref, yr_ref, on_ref, mg_ref, gt_ref, sh_ref, sc_ref, g2_ref, wr_ref, wn_ref, wo_ref,
                   wrt_ref, brt_ref, x1_ref, h2_ref, rt_ref):
    y_ret = _dot(yr_ref[...], wr_ref[...])
    y_nsa = _dot(on_ref[...].astype(BF16), wn_ref[...])
    mg = mg_ref[...].astype(F32)
    z = mg[:, :D_MODEL] * y_ret + mg[:, D_MODEL:] * y_nsa
    x1 = x_ref[...] + gt_ref[...] * _dot(z.astype(BF16), wo_ref[...])
    x1_ref[...] = x1
    h = x1 * lax.rsqrt(jnp.mean(x1 * x1, axis=-1, keepdims=True) + RMS_EPS) * g2_ref[...]
    h2 = h * (1.0 + sc_ref[...]) + sh_ref[...]
    h2_ref[...] = h2
    rt_ref[...] = _route(_dot3(h2, wrt_ref[...]) + brt_ref[...])


def _mixout(x, yret, onsa, mg, gate, shift, scale, g2, w_ret_o, w_nsa_o, w_out, w_rt, b_rt, *, tm, rows_per_mod):
    t, d = x.shape
    r = gate.shape[1]
    row = lambda cols: pl.BlockSpec((tm, cols), lambda i: (i, 0))
    mod_spec = pl.BlockSpec((None, r, d), lambda i: (i // rows_per_mod, 0, 0))
    full = lambda a: pl.BlockSpec(a.shape, lambda i: (0,) * a.ndim)
    return pl.pallas_call(
        _mixout_kernel,
        grid=(t // tm,),
        in_specs=[row(d), row(D_RET_V), row(D_NSA), row(2 * d), mod_spec, mod_spec, mod_spec,
                  pl.BlockSpec((1, d), lambda i: (0, 0)), full(w_ret_o), full(w_nsa_o), full(w_out),
                  full(w_rt), full(b_rt)],
        out_specs=[row(d), row(d), row(LANES)],
        out_shape=[jax.ShapeDtypeStruct((t, d), F32), jax.ShapeDtypeStruct((t, d), F32),
                   jax.ShapeDtypeStruct((t, LANES), F32)],
        compiler_params=_params("arbitrary"),
        name="mixout",
    )(x, yret, onsa, mg, gate, shift, scale, g2.reshape(1, d), w_ret_o, w_nsa_o, w_out, w_rt, b_rt)


def _moe_kernel(be_ref, rt_ref, nu_ref, x_hbm, w1_ref, w3_ref, w2_ref, y_ref, xbuf, sem, *, bm):
    i = pl.program_id(0)
    n_used = nu_ref[0]

    def row_copy(blk, slot, r):
        tok = rt_ref[blk * bm + r]
        return pltpu.make_async_copy(x_hbm.at[pl.ds(tok, 1), :], xbuf.at[slot, pl.ds(r, 1), :], sem.at[slot])

    def issue(blk, slot):
        def body(r, c):
            row_copy(blk, slot, r).start()
            return c
        lax.fori_loop(0, bm, body, 0)

    @pl.when((i == 0) & (n_used > 0))
    def _():
        issue(0, 0)

    @pl.when(i + 1 < n_used)
    def _():
        issue(i + 1, (i + 1) % 2)

    @pl.when(i < n_used)
    def _():
        slot = i % 2
        pltpu.make_async_copy(x_hbm.at[pl.ds(0, bm), :], xbuf.at[slot], sem.at[slot]).wait()
        xb = xbuf[slot].astype(BF16)
        hid = _silu(_dot(xb, w1_ref[...])) * _dot(xb, w3_ref[...])
        y_ref[...] = _dot(hid.astype(BF16), w2_ref[...])

    @pl.when(i >= n_used)
    def _():
        y_ref[...] = jnp.zeros(y_ref.shape, F32)


def _moe_experts(h2, blk_expert, row_tok, n_used, w_e1, w_e3, w_e2, *, bm):
    t, d = h2.shape
    n_blk = blk_expert.shape[0]
    de = w_e1.shape[-1]
    grid_spec = pltpu.PrefetchScalarGridSpec(
        num_scalar_prefetch=3,
        grid=(n_blk,),
        in_specs=[pl.BlockSpec(memory_space=pl.ANY),
                  pl.BlockSpec((None, d, de), lambda i, be, rt, nu: (be[i], 0, 0)),
                  pl.BlockSpec((None, d, de), lambda i, be, rt, nu: (be[i], 0, 0)),
                  pl.BlockSpec((None, de, d), lambda i, be, rt, nu: (be[i], 0, 0))],
        out_specs=pl.BlockSpec((bm, d), lambda i, be, rt, nu: (i, 0)),
        scratch_shapes=[pltpu.VMEM((2, bm, d), F32), pltpu.SemaphoreType.DMA((2,))],
    )
    return pl.pallas_call(
        functools.partial(_moe_kernel, bm=bm),
        grid_spec=grid_spec,
        out_shape=jax.ShapeDtypeStruct((n_blk * bm, d), F32),
        compiler_params=_params("arbitrary"),
        name="moe_experts",
    )(blk_expert, row_tok, n_used, h2, w_e1, w_e3, w_e2)


def _dispatch_plan(expert, *, bm):
    t = expert.shape[0]
    tk = t * EXPERT_TOPK
    e_flat = expert.reshape(-1)
    onehot = (e_flat[:, None] == jnp.arange(N_EXPERTS, dtype=I32)[None, :]).astype(I32)
    csum = jnp.cumsum(onehot, axis=0)
    counts = csum[-1]
    rank = jnp.sum((csum - onehot) * onehot, axis=1)
    padded = (counts + bm - 1) // bm * bm
    pad_end = jnp.cumsum(padded)
    pad_start = pad_end - padded
    dest = pad_start[e_flat] + rank
    n_blk = -(-tk // bm) + N_EXPERTS
    blk_expert = jnp.minimum(jnp.searchsorted(pad_end, jnp.arange(n_blk, dtype=I32) * bm, side='right'),
                             N_EXPERTS - 1).astype(I32)
    row_tok = jnp.zeros((n_blk * bm,), I32).at[dest].set(jnp.arange(tk, dtype=I32) // EXPERT_TOPK)
    n_used = (pad_end[-1:] // bm).astype(I32)
    return dest.reshape(t, EXPERT_TOPK), blk_expert, row_tok, n_used


def _final_kernel(ds_ref, x1_ref, rt_ref, gt_ref, gf_ref, y_hbm, o_ref, ybuf, sem, *, tm, row0):
    i = pl.program_id(0)
    n = pl.num_programs(0)

    def issue(blk, slot):
        def body(r, c):
            for k in range(EXPERT_TOPK):
                src = ds_ref[(row0 + blk * tm + r) * EXPERT_TOPK + k]
                pltpu.make_async_copy(y_hbm.at[pl.ds(src, 1), :], ybuf.at[slot, k, pl.ds(r, 1), :],
                                      sem.at[slot]).start()
            return c
        lax.fori_loop(0, tm, body, 0)

    @pl.when(i == 0)
    def _():
        issue(0, 0)

    @pl.when(i + 1 < n)
    def _():
        issue(i + 1, (i + 1) % 2)

    slot = i % 2
    for k in range(EXPERT_TOPK):
        pltpu.make_async_copy(y_hbm.at[pl.ds(0, tm), :], ybuf.at[slot, k], sem.at[slot]).wait()
    rt = rt_ref[...]
    moe = rt[:, 2:3] * ybuf[slot, 0] + rt[:, 3:4] * ybuf[slot, 1]
    x2 = x1_ref[...] + gt_ref[...] * moe
    o_ref[...] = x2 * lax.rsqrt(jnp.mean(x2 * x2, axis=-1, keepdims=True) + RMS_EPS) * gf_ref[...]


def _final(dest_flat, x1, route, gate, g_final, y_buf, *, tm, rows_per_mod, row0):
    t, d = x1.shape
    r = gate.shape[1]
    grid_spec = pltpu.PrefetchScalarGridSpec(
        num_scalar_prefetch=1,
        grid=(t // tm,),
        in_specs=[pl.BlockSpec((tm, d), lambda i, ds: (i, 0)), pl.BlockSpec((tm, LANES), lambda i, ds: (i, 0)),
                  pl.BlockSpec((None, r, d), lambda i, ds: (i // rows_per_mod, 0, 0)),
                  pl.BlockSpec((1, d), lambda i, ds: (0, 0)), pl.BlockSpec(memory_space=pl.ANY)],
        out_specs=pl.BlockSpec((tm, d), lambda i, ds: (i, 0)),
        scratch_shapes=[pltpu.VMEM((2, EXPERT_TOPK, tm, d), F32), pltpu.SemaphoreType.DMA((2,))],
    )
    return pl.pallas_call(
        functools.partial(_final_kernel, tm=tm, row0=row0),
        grid_spec=grid_spec,
        out_shape=jax.ShapeDtypeStruct((t, d), F32),
        compiler_params=_params("arbitrary"),
        name="final",
    )(dest_flat, x1, route, gate, g_final.reshape(1, d), y_buf)


def kernel(x_prompt, x_sample, c_prompt, c_sample, state_ret, cache_cmp_kv, cache_sel_kv, cache_win_kv, page_table,
           w_ada, b_ada, g_norm1, g_norm2, g_final, w_in, ret_norm_g, cmp_w1, cmp_pe, cmp_w2, w_ret_o, w_nsa_o,
           w_out, w_grp, b_grp, w_exp, b_exp, w_e1, w_e3, w_e2):
    depth = w_in.shape[0]
    assert depth == 1, "one layer"
    batch, seq, d = x_prompt.shape
    db, n_tok, _ = x_sample.shape
    n_pages = page_table.shape[1]
    past = n_pages * PAGE_SIZE
    win_buf = cache_win_kv.shape[2]
    assert win_buf <= seq
    tp = batch * seq
    ts = db * n_tok
    l = 0

    mod = _adaln(jnp.concatenate([c_prompt, c_sample], axis=0).astype(F32), w_ada[l], b_ada[l])
    mods = jnp.split(mod, 6, axis=-1)
    mod_p = [m[:batch].reshape(batch, 1, d) for m in mods]
    ts_tile = min(ts, 256)
    mod_s = [jnp.repeat(m[batch:], n_tok, axis=0).reshape(ts // ts_tile, ts_tile, d) for m in mods]

    w_pad = _pad_w_in(w_in[l])
    perm, w1c, w2bd, pe8, w1f = _cmp_weights(cmp_w1[l], cmp_pe[l], cmp_w2[l])
    wr, wn, wo = w_ret_o[l].astype(BF16), w_nsa_o[l].astype(BF16), w_out[l].astype(BF16)
    w_rt = jnp.concatenate([w_grp[l], w_exp[l], jnp.zeros((d, LANES - N_GROUPS - N_EXPERTS), F32)], axis=1)
    b_rt = jnp.concatenate([b_grp[l], b_exp[l], jnp.zeros((LANES - N_GROUPS - N_EXPERTS,), F32)]).reshape(1, LANES)

    tm_p = 256
    cos_p, sin_p = _rope_tables(jnp.arange(seq))
    (rq, rk, rv, rg, nq, ckv_p, skv_p, wkv_p, skb, wkb, ng, mg) = _inproj(
        x_prompt.reshape(tp, d).astype(F32), mod_p[0], mod_p[1], g_norm1[l], cos_p, sin_p, w_pad,
        tm=tm_p, rows_per_mod=seq // tm_p, pos_blocks=seq // tm_p, act_dtype=BF16)
    yret_p, ret_state_p = _ret_prompt(rq, rk, rv, rg, ret_norm_g[l], batch=batch, seq=seq)
    ck, cv = _cmp_prompt(ckv_p, perm, w1c, w2bd, pe8, w1f, batch=batch, seq=seq)
    onsa_p = _nsa_prompt(nq, ng, ck, cv, skb, wkb, batch=batch, seq=seq)
    tm_m = 512
    x1_p, h2_p, route_p = _mixout(x_prompt.reshape(tp, d).astype(F32), yret_p, onsa_p, mg, mod_p[2], mod_p[3],
                                  mod_p[4], g_norm2[l], wr, wn, wo, w_rt, b_rt, tm=tm_m, rows_per_mod=seq // tm_m)

    cos_s, sin_s = _rope_tables(past + jnp.arange(n_tok))
    cos_s = jnp.tile(cos_s, (ts_tile // n_tok, 1))
    sin_s = jnp.tile(sin_s, (ts_tile // n_tok, 1))
    (rq_s, rk_s, rv_s, rg_s, nq_s, ckv_s, skv_s, wkv_s, _, _, ng_s, mg_s) = _inproj(
        x_sample.reshape(ts, d).astype(F32), mod_s[0], mod_s[1], g_norm1[l], cos_s, sin_s, w_pad,
        tm=ts_tile, rows_per_mod=1, pos_blocks=1, act_dtype=F32)
    yret_s, ret_state_s = _ret_sample(rq_s, rk_s, rv_s, rg_s, state_ret[l].astype(F32), ret_norm_g[l], n_tok=n_tok)
    n_phys = cache_cmp_kv.shape[1]
    onsa_s, win_s = _nsa_sample(
        nq_s, ng_s, skv_s, wkv_s,
        cache_cmp_kv[l].reshape(n_phys, PAGE_SIZE, D_KV).astype(F32),
        cache_sel_kv[l].reshape(n_phys, PAGE_SIZE, D_KV).astype(F32),
        cache_win_kv[l].reshape(db, win_buf, D_KV).astype(F32), page_table, perm, w1c, w2bd, pe8, w1f, n_tok=n_tok)
    x1_s, h2_s, route_s = _mixout(x_sample.reshape(ts, d).astype(F32), yret_s, onsa_s, mg_s, mod_s[2], mod_s[3],
                                  mod_s[4], g_norm2[l], wr, wn, wo, w_rt, b_rt, tm=ts_tile, rows_per_mod=1)

    bm = 256
    h2 = jnp.concatenate([h2_p, h2_s], axis=0)
    route = jnp.concatenate([route_p, route_s], axis=0)
    expert = route[:, :EXPERT_TOPK].astype(I32)
    dest, blk_expert, row_tok, n_used = _dispatch_plan(expert, bm=bm)
    y_buf = _moe_experts(h2, blk_expert, row_tok, n_used,
                         w_e1[l].astype(BF16), w_e3[l].astype(BF16), w_e2[l].astype(BF16), bm=bm)
    dest_flat = dest.reshape(-1)
    y_p = _final(dest_flat, x1_p, route_p, mod_p[5], g_final, y_buf, tm=tm_p, rows_per_mod=seq // tm_p, row0=0)
    y_s = _final(dest_flat, x1_s, route_s, mod_s[5], g_final, y_buf, tm=ts_tile, rows_per_mod=1, row0=tp)

    kv6 = lambda a, n, s: a.reshape(1, n, s, 2, NSA_GROUPS, NSA_HEAD_DIM)
    win_p = wkv_p.reshape(batch, seq, D_KV)[:, seq - win_buf:]
    return (y_p.reshape(batch, seq, d).astype(x_prompt.dtype), y_s.reshape(db, n_tok, d).astype(x_sample.dtype),
            ret_state_p[None].astype(state_ret.dtype), ret_state_s[None].astype(state_ret.dtype),
            kv6(ckv_p, batch, seq).astype(cache_cmp_kv.dtype), kv6(ckv_s, db, n_tok).astype(cache_cmp_kv.dtype),
            kv6(skv_p, batch, seq).astype(cache_sel_kv.dtype), kv6(skv_s, db, n_tok).astype(cache_sel_kv.dtype),
            kv6(win_p, batch, win_buf).astype(cache_win_kv.dtype), kv6(win_s, db, win_buf).astype(cache_win_kv.dtype))
```

```python
import functools

import numpy as np
import jax
import jax.numpy as jnp
from jax import lax
from jax.experimental import pallas as pl
from jax.experimental.pallas import tpu as pltpu

F32 = jnp.float32
BF16 = jnp.bfloat16
I32 = jnp.int32

D_MODEL = 1024
PAGE_SIZE = 128
RET_HEADS = 8
RET_DK = 64
RET_DV = 128
ROPE_BASE = 10000.0
NSA_HEADS = 16
NSA_GROUPS = 4
NSA_HPG = NSA_HEADS // NSA_GROUPS
NSA_HEAD_DIM = 64
CMP_LEN = 32
CMP_STRIDE = 16
CMP_HID = 128
SEL_BLOCK = 64
SEL_TOPK = 16
WINDOW = 512
N_GROUPS = 4
EXPERTS_PER_GROUP = 8
N_EXPERTS = N_GROUPS * EXPERTS_PER_GROUP
EXPERT_TOPK = 2
D_EXPERT = 512
RMS_EPS = 1e-6
NEG_INF = -1e30
FORCE_SCORE = 1e9

LANES = 128
D_RET_QK = RET_HEADS * RET_DK
D_RET_V = RET_HEADS * RET_DV
D_NSA = NSA_HEADS * NSA_HEAD_DIM
D_KV = 2 * NSA_GROUPS * NSA_HEAD_DIM
D_GK = NSA_GROUPS * NSA_HEAD_DIM
N_GATE = 3 * NSA_HEADS
C_RQ = 0
C_RK = C_RQ + D_RET_QK
C_RV = C_RK + D_RET_QK
C_RG = C_RV + D_RET_V
C_NQ = C_RG + D_RET_V
C_CKV = C_NQ + D_NSA
C_SKV = C_CKV + D_KV
C_WKV = C_SKV + D_KV
C_NG = C_WKV + D_KV
C_MG = C_NG + LANES
C_END = C_MG + 2 * D_MODEL
VMEM_LIMIT = 56 << 20


def _dot(a, b):
    return jnp.dot(a, b, preferred_element_type=F32)


def _dot_nt(a, b):
    return lax.dot_general(a, b, (((1,), (1,)), ((), ())), preferred_element_type=F32)


def _dot_tn(a, b):
    return lax.dot_general(a, b, (((0,), (0,)), ((), ())), preferred_element_type=F32)


def _split(x):
    hi = x.astype(BF16)
    return hi, (x - hi.astype(F32)).astype(BF16)


def _dot3(a, w):
    ah, al = _split(a)
    wh, wl = _split(w)
    return _dot(ah, wh) + _dot(ah, wl) + _dot(al, wh)


def _silu(x):
    return x * jax.nn.sigmoid(x)


def _params(*sem):
    return pltpu.CompilerParams(dimension_semantics=sem, vmem_limit_bytes=VMEM_LIMIT)


def _adaln_kernel(c_ref, w_ref, b_ref, o_ref):
    o_ref[...] = _dot3(_silu(c_ref[...]), w_ref[...]) + b_ref[...]


def _adaln(c, w, b):
    r, d = c.shape
    n = w.shape[1]
    tn = 1024
    return pl.pallas_call(
        _adaln_kernel,
        grid=(n // tn,),
        in_specs=[pl.BlockSpec((r, d), lambda j: (0, 0)),
                  pl.BlockSpec((d, tn), lambda j: (0, j)),
                  pl.BlockSpec((1, tn), lambda j: (0, j))],
        out_specs=pl.BlockSpec((r, tn), lambda j: (0, j)),
        out_shape=jax.ShapeDtypeStruct((r, n), F32),
        compiler_params=_params("arbitrary"),
        name="adaln",
    )(c, w, b.reshape(1, n))


def _rope(x, cos, sin):
    outs = []
    lane = lax.broadcasted_iota(I32, cos.shape, 1)
    first = (lane & 32) == 0
    for j in range(x.shape[1] // LANES):
        xj = x[:, j * LANES:(j + 1) * LANES]
        partner = jnp.where(first, pltpu.roll(xj, LANES - 32, 1), pltpu.roll(xj, 32, 1))
        outs.append(xj * cos + partner * sin)
    return jnp.concatenate(outs, axis=1)


def _inproj_kernel(x_ref, sh_ref, sc_ref, g_ref, cos_ref, sin_ref, w_ref, *rest, key_major):
    if key_major:
        (wt_ref, rq_ref, rk_ref, rv_ref, rg_ref, nq_ref, ckv_ref, skv_ref, wkv_ref, skb_ref, wkb_ref, ng_ref,
         mg_ref, svt_ref, wvt_ref) = rest
    else:
        (rq_ref, rk_ref, rv_ref, rg_ref, nq_ref, ckv_ref, skv_ref, wkv_ref, skb_ref, wkb_ref, ng_ref,
         mg_ref) = rest
    x = x_ref[...]
    h = x * lax.rsqrt(jnp.mean(x * x, axis=-1, keepdims=True) + RMS_EPS) * g_ref[...]
    hb = (h * (1.0 + sc_ref[...]) + sh_ref[...]).astype(BF16)

    def proj(a, b):
        return _dot(hb, w_ref[:, a:b])

    cos = cos_ref[...]
    sin = sin_ref[...]
    rq_ref[...] = _rope(proj(C_RQ, C_RK), cos, sin).astype(rq_ref.dtype)
    rk_ref[...] = (_rope(proj(C_RK, C_RV), cos, sin) * (RET_DK ** -0.5)).astype(rk_ref.dtype)
    rv_ref[...] = proj(C_RV, C_RG).astype(rv_ref.dtype)
    rg_ref[...] = proj(C_RG, C_NQ).astype(rg_ref.dtype)
    if key_major:
        n_t = nq_ref.shape[0]

        def proj_t(a, b, o_ref, scale):
            y = _dot_nt(wt_ref[a:b, :], hb) * scale
            for k in range(n_t):
                o_ref[k] = y[:, k * LANES:(k + 1) * LANES].astype(o_ref.dtype)

        proj_t(0, D_NSA, nq_ref, NSA_HEAD_DIM ** -0.5)
        proj_t(D_NSA, D_NSA + D_GK, svt_ref, 1.0)
        proj_t(D_NSA + D_GK, D_NSA + 2 * D_GK, wvt_ref, 1.0)
    else:
        nq_ref[...] = (proj(C_NQ, C_CKV) * (NSA_HEAD_DIM ** -0.5)).astype(nq_ref.dtype)
    ckv_ref[...] = proj(C_CKV, C_SKV)
    skv = proj(C_SKV, C_WKV)
    skv_ref[...] = skv
    skb_ref[...] = skv.astype(BF16)
    wkv = proj(C_WKV, C_NG)
    wkv_ref[...] = wkv
    wkb_ref[...] = wkv.astype(BF16)
    ng_ref[...] = jax.nn.sigmoid(proj(C_NG, C_MG))
    mg_ref[...] = jax.nn.sigmoid(proj(C_MG, C_END)).astype(BF16)


def _inproj(x, shift, scale, g, cos, sin, w_pad, w_t, *, tm, rows_per_mod, pos_blocks, act_dtype):
    t, d = x.shape
    r = shift.shape[1]
    key_major = w_t is not None
    mod_spec = pl.BlockSpec((None, r, d), lambda i: (i // rows_per_mod, 0, 0))
    pos_spec = pl.BlockSpec((tm, LANES), lambda i: (i % pos_blocks, 0))

    def out(cols, dt):
        return pl.BlockSpec((tm, cols), lambda i: (i, 0)), jax.ShapeDtypeStruct((t, cols), dt)

    def out_t(rows):
        return (pl.BlockSpec((tm // LANES, rows, LANES), lambda i: (i, 0, 0)),
                jax.ShapeDtypeStruct((t // LANES, rows, LANES), BF16))

    outs = [out(D_RET_QK, act_dtype), out(D_RET_QK, act_dtype), out(D_RET_V, act_dtype), out(D_RET_V, act_dtype),
            out_t(D_NSA) if key_major else out(D_NSA, act_dtype), out(D_KV, F32), out(D_KV, F32), out(D_KV, F32),
            out(D_KV, BF16), out(D_KV, BF16), out(LANES, F32), out(2 * D_MODEL, BF16)]
    in_specs = [pl.BlockSpec((tm, d), lambda i: (i, 0)), mod_spec, mod_spec,
                pl.BlockSpec((1, d), lambda i: (0, 0)), pos_spec, pos_spec,
                pl.BlockSpec((d, C_END), lambda i: (0, 0), pipeline_mode=pl.Buffered(1))]
    args = [x, shift, scale, g.reshape(1, d), cos, sin, w_pad]
    if key_major:
        outs += [out_t(D_GK), out_t(D_GK)]
        in_specs.append(pl.BlockSpec(w_t.shape, lambda i: (0, 0), pipeline_mode=pl.Buffered(1)))
        args.append(w_t)
    return pl.pallas_call(
        functools.partial(_inproj_kernel, key_major=key_major),
        grid=(t // tm,),
        in_specs=in_specs,
        out_specs=[o[0] for o in outs],
        out_shape=[o[1] for o in outs],
        compiler_params=_params("arbitrary"),
        name="inproj",
    )(*args)


def _rope_tables(pos):
    half = RET_DK // 2
    inv_freq = ROPE_BASE ** (-jnp.arange(half, dtype=F32) / half)
    ang = pos.astype(F32)[:, None] * inv_freq[None, :]
    cos, sin = jnp.cos(ang), jnp.sin(ang)
    return jnp.tile(cos, (1, 4)), jnp.tile(jnp.concatenate([-sin, sin], axis=1), (1, 2))


def _pad_w_in(w_in):
    d = w_in.shape[0]
    n_real = C_NG + N_GATE
    return jnp.concatenate([w_in[:, :n_real], jnp.zeros((d, LANES - N_GATE), w_in.dtype), w_in[:, n_real:]],
                           axis=1).astype(BF16)


def _ret_log_decay():
    return np.log1p(-np.exp2(-5.0 - np.arange(RET_HEADS, dtype=np.float32))).astype(np.float32)


def _ret_head_out(o, g, gn):
    n = o * lax.rsqrt(jnp.mean(o * o, axis=-1, keepdims=True) + RMS_EPS) * gn
    return _silu(g) * n


def _ret_prompt_kernel(q_ref, k_ref, v_ref, g_ref, dm_ref, qd_ref, kd_ref, gn_ref, y_ref, sf_ref, s_scr, *, sdec):
    j = pl.program_id(1)

    @pl.when(j == 0)
    def _():
        s_scr[...] = jnp.zeros_like(s_scr)

    qb = q_ref[...]
    kb = k_ref[...]
    qd = (qb.astype(F32) * qd_ref[...]).astype(BF16)
    kd = (kb.astype(F32) * kd_ref[...]).astype(BF16)
    gn = gn_ref[...]
    for h in range(RET_HEADS):
        sl = slice(h * RET_DK, (h + 1) * RET_DK)
        vl = slice(h * RET_DV, (h + 1) * RET_DV)
        vh = v_ref[:, vl]
        s_h = s_scr[h]
        sc = _dot_nt(qb[:, sl], kb[:, sl]) * dm_ref[h]
        o = _dot(sc.astype(BF16), vh) + _dot(qd[:, sl], s_h.astype(BF16))
        s_scr[h] = sdec[h] * s_h + _dot_tn(kd[:, sl], vh)
        y_ref[:, vl] = _ret_head_out(o, g_ref[:, vl].astype(F32), gn).astype(y_ref.dtype)

    @pl.when(j == pl.num_programs(1) - 1)
    def _():
        sf_ref[...] = s_scr[...]


def _ret_tables(c):
    lg = _ret_log_decay()
    i = np.arange(c, dtype=np.float32)
    diff = i[:, None] - i[None, :]
    dm = np.where(diff[None] >= 0, np.exp(lg[:, None, None] * np.maximum(diff, 0.0)[None]), 0.0).astype(np.float32)
    qd = np.repeat(np.exp(lg[None, :] * (i[:, None] + 1.0)), RET_DK, axis=1).astype(np.float32)
    kd = np.repeat(np.exp(lg[None, :] * (c - 1.0 - i)[:, None]), RET_DK, axis=1).astype(np.float32)
    sdec = tuple(float(v) for v in np.exp(lg * np.float32(c)))
    return dm, qd, kd, sdec


def _ret_prompt(rq, rk, rv, rg, gn, *, batch, seq):
    c = 128
    n = seq // c
    dm, qd, kd, sdec = _ret_tables(c)
    row = lambda b, j: (b * n + j, 0)
    const2 = lambda b, j: (0, 0)
    return pl.pallas_call(
        functools.partial(_ret_prompt_kernel, sdec=sdec),
        grid=(batch, n),
        in_specs=[pl.BlockSpec((c, D_RET_QK), row), pl.BlockSpec((c, D_RET_QK), row),
                  pl.BlockSpec((c, D_RET_V), row), pl.BlockSpec((c, D_RET_V), row),
                  pl.BlockSpec((RET_HEADS, c, c), lambda b, j: (0, 0, 0)),
                  pl.BlockSpec((c, D_RET_QK), const2), pl.BlockSpec((c, D_RET_QK), const2),
                  pl.BlockSpec((1, RET_DV), const2)],
        out_specs=[pl.BlockSpec((c, D_RET_V), row),
                   pl.BlockSpec((None, RET_HEADS, RET_DK, RET_DV), lambda b, j: (b, 0, 0, 0))],
        out_shape=[jax.ShapeDtypeStruct((batch * seq, D_RET_V), BF16),
                   jax.ShapeDtypeStruct((batch, RET_HEADS, RET_DK, RET_DV), F32)],
        scratch_shapes=[pltpu.VMEM((RET_HEADS, RET_DK, RET_DV), F32)],
        compiler_params=_params("arbitrary", "arbitrary"),
        name="ret_prompt",
    )(rq, rk, rv, rg, jnp.asarray(dm), jnp.asarray(qd), jnp.asarray(kd), gn.reshape(1, RET_DV))


def _ret_sample_kernel(q_ref, k_ref, v_ref, g_ref, s0_ref, dm_ref, qd_ref, kd_ref, gn_ref, y_ref, sn_ref,
                       *, sdec, n_req, n_tok):
    rows = n_req * n_tok
    q = q_ref[...]
    k = k_ref[...]
    qd = q * qd_ref[...]
    kd = k * kd_ref[...]
    qb = q.astype(BF16)
    kb = k.astype(BF16)
    gn = gn_ref[...]
    req = lax.broadcasted_iota(I32, (rows, 1), 0) // n_tok
    for h in range(RET_HEADS):
        sl = slice(h * RET_DK, (h + 1) * RET_DK)
        vl = slice(h * RET_DV, (h + 1) * RET_DV)
        vh = v_ref[:, vl].astype(BF16)
        sc = _dot_nt(qb[:, sl], kb[:, sl]) * dm_ref[h]
        o = _dot(sc.astype(BF16), vh)
        qd_h = qd[:, sl]
        kd_h = kd[:, sl]
        for r in range(n_req):
            own = req == r
            s_r = s0_ref[r, h]
            o = o + _dot(jnp.where(own, qd_h, 0.0).astype(BF16), s_r.astype(BF16))
            sn_ref[r, h] = sdec[h] * s_r + _dot_tn(jnp.where(own, kd_h, 0.0).astype(BF16), vh)
        y_ref[:, vl] = _ret_head_out(o, g_ref[:, vl], gn).astype(y_ref.dtype)


def _ret_sample(rq, rk, rv, rg, state, gn, *, n_tok):
    db = state.shape[0]
    n_req = 16
    rows = n_req * n_tok
    lg = _ret_log_decay()
    i = np.arange(rows)
    t = (i % n_tok).astype(np.float32)
    same = (i[:, None] // n_tok) == (i[None, :] // n_tok)
    diff = t[:, None] - t[None, :]
    dm = np.where((same & (diff >= 0))[None], np.exp(lg[:, None, None] * np.maximum(diff, 0.0)[None]), 0.0)
    qd = np.repeat(np.exp(lg[None, :] * (t[:, None] + 1.0)), RET_DK, axis=1)
    kd = np.repeat(np.exp(lg[None, :] * (n_tok - 1.0 - t)[:, None]), RET_DK, axis=1)
    sdec = tuple(float(v) for v in np.exp(lg * np.float32(n_tok)))
    row = lambda i: (i, 0)
    const2 = lambda i: (0, 0)
    st_spec = pl.BlockSpec((n_req, RET_HEADS, RET_DK, RET_DV), lambda i: (i, 0, 0, 0))
    return pl.pallas_call(
        functools.partial(_ret_sample_kernel, sdec=sdec, n_req=n_req, n_tok=n_tok),
        grid=(db // n_req,),
        in_specs=[pl.BlockSpec((rows, D_RET_QK), row), pl.BlockSpec((rows, D_RET_QK), row),
                  pl.BlockSpec((rows, D_RET_V), row), pl.BlockSpec((rows, D_RET_V), row), st_spec,
                  pl.BlockSpec((RET_HEADS, rows, rows), lambda i: (0, 0, 0)),
                  pl.BlockSpec((rows, D_RET_QK), const2), pl.BlockSpec((rows, D_RET_QK), const2),
                  pl.BlockSpec((1, RET_DV), const2)],
        out_specs=[pl.BlockSpec((rows, D_RET_V), row), st_spec],
        out_shape=[jax.ShapeDtypeStruct((db * n_tok, D_RET_V), BF16),
                   jax.ShapeDtypeStruct(state.shape, F32)],
        compiler_params=_params("arbitrary"),
        name="ret_sample",
    )(rq, rk, rv, rg, state, jnp.asarray(dm, F32), jnp.asarray(qd, F32), jnp.asarray(kd, F32),
      gn.reshape(1, RET_DV))


def _cmp_weights(cmp_w1, cmp_pe, cmp_w2):
    w1c = jnp.concatenate([cmp_w1[:, :CMP_STRIDE], cmp_w1[:, CMP_STRIDE:]], axis=-1).astype(BF16)
    eye = jnp.eye(NSA_GROUPS, dtype=cmp_w2.dtype)
    w2bd = jnp.einsum('gk,chd->cghkd', eye, cmp_w2).reshape(2, NSA_GROUPS * CMP_HID, D_GK).astype(BF16)
    pe8 = jnp.broadcast_to(cmp_pe.reshape(2, 1, CMP_LEN * NSA_HEAD_DIM), (2, 8, CMP_LEN * NSA_HEAD_DIM))
    w1f = cmp_w1.reshape(2, CMP_LEN * NSA_HEAD_DIM, CMP_HID)
    r = np.arange(PAGE_SIZE)
    seg_pp = PAGE_SIZE // CMP_STRIDE
    perm = r[None, :] == (r[:, None] % seg_pp) * CMP_STRIDE + r[:, None] // seg_pp
    return jnp.asarray(perm, BF16), w1c, w2bd, jnp.swapaxes(w2bd, 1, 2), pe8, w1f


def _cmp_pe_term(pe8_ref, w1f_ref, c):
    return _dot3(pe8_ref[c], w1f_ref[c])[0:1]


def _cmp_partial(pages, perm, w1c_ref, c, feature_major):
    seg_pp = PAGE_SIZE // CMP_STRIDE
    if feature_major:
        ys = [_dot_nt(perm, pg[c * D_GK:(c + 1) * D_GK, :].astype(BF16)) for pg in pages]
    else:
        ys = [_dot(perm, pg[:, c * D_GK:(c + 1) * D_GK].astype(BF16)) for pg in pages]
    acc = None
    for p in range(CMP_STRIDE):
        xp = jnp.concatenate([y[p * seg_pp:(p + 1) * seg_pp] for y in ys], axis=0)
        xs = jnp.concatenate([xp[:, g * NSA_HEAD_DIM:(g + 1) * NSA_HEAD_DIM] for g in range(NSA_GROUPS)], axis=0)
        d = _dot(xs.astype(BF16), w1c_ref[c, p])
        acc = d if acc is None else acc + d
    return acc


def _cmp_store(pages, perm, w1c_ref, lohi_scr, step, feature_major=False):
    seg_ps = len(pages) * (PAGE_SIZE // CMP_STRIDE)
    off = pl.multiple_of(step * seg_ps, seg_ps)
    for c in range(2):
        acc = _cmp_partial(pages, perm, w1c_ref, c, feature_major)
        for g in range(NSA_GROUPS):
            lohi_scr[c, g, pl.ds(off, seg_ps), :] = acc[g * seg_ps:(g + 1) * seg_ps]


def _cmp_finish(lohi, pe, w2_c, feature_major):
    hs = []
    for g in range(NSA_GROUPS):
        lh = lohi(g)
        nseg = lh.shape[0]
        z = lh[:, :CMP_HID] + pltpu.roll(lh[:, CMP_HID:], nseg - 1, 0) + pe
        hs.append(_silu(z).astype(BF16))
    hid = jnp.concatenate(hs, axis=1)
    if feature_major:
        return _dot_nt(w2_c, hid)
    return _dot(hid, w2_c)


def _q_blockdiag(qf, tq):
    lane = lax.broadcasted_iota(I32, (tq, D_GK), 1)
    blocks = []
    for g in range(NSA_GROUPS):
        slab = qf[:, g * D_GK:(g + 1) * D_GK]
        keep = (lane >= g * NSA_HEAD_DIM) & (lane < (g + 1) * NSA_HEAD_DIM)
        for r in range(NSA_HPG):
            sh = (NSA_HEAD_DIM * (g - r)) % D_GK
            rolled = slab if sh == 0 else pltpu.roll(slab, sh, 1)
            blocks.append(jnp.where(keep, rolled, 0.0))
    return jnp.concatenate(blocks, axis=0).astype(BF16)


def _pair_pv(p, v):
    half = p.shape[0] // 2
    return jnp.concatenate([_dot(p[:half], v[:, :LANES]), _dot(p[half:], v[:, LANES:])], axis=0)


def _assemble_heads(acc, tq):
    lane = lax.broadcasted_iota(I32, (tq, LANES), 1)
    chunks = []
    for c in range(NSA_HEADS // 2):
        a = acc[(2 * c) * tq:(2 * c + 1) * tq]
        b = acc[(2 * c + 1) * tq:(2 * c + 2) * tq]
        if (c // 2) % 2 == 0:
            b = pltpu.roll(b, NSA_HEAD_DIM, 1)
        else:
            a = pltpu.roll(a, NSA_HEAD_DIM, 1)
        chunks.append(jnp.where(lane < NSA_HEAD_DIM, a, b))
    return jnp.concatenate(chunks, axis=1)


def _gate_branches(ng, eg_ref, branches, tq):
    nh, nl = _split(ng)
    out = None
    for br, acc in enumerate(branches):
        gexp = _dot(nh, eg_ref[br]) + _dot(nl, eg_ref[br])
        term = gexp * _assemble_heads(acc, tq)
        out = term if out is None else out + term
    return out


def _masked_softmax(s, mask):
    s = jnp.where(mask, s, NEG_INF)
    m = jnp.max(s, axis=-1, keepdims=True)
    e = jnp.exp(s - m)
    return e / jnp.sum(e, axis=-1, keepdims=True)


def _rank_select(sc, sc_scr, valid, n_rows):
    sc_scr[...] = sc
    n_io = lax.broadcasted_iota(I32, sc.shape, 0)

    def body(m, cnt):
        row = sc_scr[pl.ds(m, 1), :]
        beats = (row > sc) | ((row == sc) & (m < n_io))
        return cnt + jnp.where(beats, 1, 0)

    cnt = lax.fori_loop(0, n_rows, body, jnp.zeros(sc.shape, I32))
    return jnp.where((cnt < SEL_TOPK) & valid, 1.0, 0.0)


def _block_scores(scT, pos, n_io):
    valid = n_io * SEL_BLOCK <= pos
    cur = pos // SEL_BLOCK
    forced = (n_io == 0) | (n_io == cur) | (n_io == cur - 1)
    return jnp.where(forced, FORCE_SCORE, jnp.where(valid, scT, -FORCE_SCORE)), valid


def _nsa_tables(n_cmp_pad, n_cmp, n_blocks, nb_pad, n_keys_pad, tk, key_major=False):
    c = np.arange(n_cmp_pad)[None, :]
    n = np.arange(nb_pad)[:, None]
    ov = np.clip(np.minimum(CMP_STRIDE * c + CMP_LEN, SEL_BLOCK * n + SEL_BLOCK)
                 - np.maximum(CMP_STRIDE * c, SEL_BLOCK * n), 0, None) / CMP_STRIDE
    ov = np.where((c < n_cmp) & (n < n_blocks), ov, 0.0)
    key = np.arange(n_keys_pad)
    e = (key[None, :] // SEL_BLOCK == np.arange(LANES * ((nb_pad + LANES - 1) // LANES))[:, None])
    e3 = e.reshape(e.shape[0], n_keys_pad // tk, tk).transpose(1, 0, 2)
    if key_major:
        e3 = e3.transpose(0, 2, 1)
    col = np.arange(LANES)[:, None]
    lane = np.arange(D_NSA)[None, :]
    eg = np.stack([col == br * NSA_HEADS + lane // NSA_HEAD_DIM for br in range(3)])
    return jnp.asarray(ov, BF16), jnp.asarray(e3, BF16), jnp.asarray(eg, BF16)


def _cmp_prompt_kernel(ck_ref, perm_ref, w1c_ref, w2_ref, w2t_ref, pe8_ref, w1f_ref, ko_ref, vto_ref, lohi_scr):
    st = pl.program_id(1)
    n_pg = ck_ref.shape[0] // PAGE_SIZE
    pages = [ck_ref[k * PAGE_SIZE:(k + 1) * PAGE_SIZE, :] for k in range(n_pg)]
    _cmp_store(pages, perm_ref[...], w1c_ref, lohi_scr, st)

    @pl.when(st == pl.num_programs(1) - 1)
    def _():
        ko_ref[...] = _cmp_finish(lambda g: lohi_scr[0, g], _cmp_pe_term(pe8_ref, w1f_ref, 0), w2_ref[0],
                                  False).astype(ko_ref.dtype)
        vto_ref[...] = _cmp_finish(lambda g: lohi_scr[1, g], _cmp_pe_term(pe8_ref, w1f_ref, 1), w2t_ref[1],
                                   True).astype(vto_ref.dtype)


def _cmp_prompt(ckv, cmp_w, *, batch, seq):
    nseg = seq // CMP_STRIDE
    rows = 8 * PAGE_SIZE
    n_st = seq // rows
    full = lambda a: pl.BlockSpec(a.shape, lambda b, s: (0,) * a.ndim)
    return pl.pallas_call(
        _cmp_prompt_kernel,
        grid=(batch, n_st),
        in_specs=[pl.BlockSpec((rows, D_KV), lambda b, s: (b * n_st + s, 0))] + [full(a) for a in cmp_w],
        out_specs=[pl.BlockSpec((nseg, D_GK), lambda b, s: (b, 0)),
                   pl.BlockSpec((None, D_GK, nseg), lambda b, s: (b, 0, 0))],
        out_shape=[jax.ShapeDtypeStruct((batch * nseg, D_GK), BF16),
                   jax.ShapeDtypeStruct((batch, D_GK, nseg), BF16)],
        scratch_shapes=[pltpu.VMEM((2, NSA_GROUPS, nseg, 2 * CMP_HID), F32)],
        compiler_params=_params("arbitrary", "arbitrary"),
        name="cmp_prompt",
    )(ckv, *cmp_w)


def _col_softmax(s, mask):
    s = jnp.where(mask, s, NEG_INF)
    e = jnp.exp(s - jnp.max(s, axis=0, keepdims=True))
    return e, jnp.sum(e, axis=0, keepdims=True)


def _nsa_prompt_kernel(qt_ref, ng_ref, ck_ref, cvt_ref, sk_ref, svt_ref, wk_ref, wvt_ref, ov_ref, et_ref,
                       o_ref, m_scr, l_scr, acc_scr, sc_scr, *, tq, tk, wlen):
    seq = sk_ref.shape[0]
    nseg = ck_ref.shape[0]
    nb = ov_ref.shape[0]
    gl = NSA_HPG * tq
    hd = NSA_HEAD_DIM
    q0 = pl.program_id(1) * tq

    qt = qt_ref[0]
    qg = []
    for g in range(NSA_GROUPS):
        cols = []
        for r in range(NSA_HPG):
            h = NSA_HPG * g + r
            parts = [qt[h * hd:(h + 1) * hd, :]]
            if g > 0:
                parts.insert(0, jnp.zeros((hd * g, tq), BF16))
            if g < NSA_GROUPS - 1:
                parts.append(jnp.zeros((hd * (NSA_GROUPS - 1 - g), tq), BF16))
            cols.append(jnp.concatenate(parts, axis=0))
        qg.append(jnp.concatenate(cols, axis=1))
    pos_t = q0 + lax.broadcasted_iota(I32, (1, gl), 1) % tq

    ck = ck_ref[...]
    cidx = lax.broadcasted_iota(I32, (nseg, gl), 0)
    cmask = (CMP_STRIDE * cidx + (CMP_LEN - 1) <= pos_t) & (cidx < nseg - 1)
    o_cmp, pgs = [], []
    for g in range(NSA_GROUPS):
        e, l = _col_softmax(_dot(ck, qg[g]), cmask)
        p = jnp.where(cmask, e * (1.0 / l), 0.0)
        o_cmp.append(_dot(cvt_ref[g * hd:(g + 1) * hd, :], p.astype(BF16)))
        pgs.append(p[:, 0:tq] + p[:, tq:2 * tq] + p[:, 2 * tq:3 * tq] + p[:, 3 * tq:4 * tq])
    ph, plo = _split(jnp.concatenate(pgs, axis=1))
    sc_t = _dot(ov_ref[...], ph) + _dot(ov_ref[...], plo)

    n_io = lax.broadcasted_iota(I32, sc_t.shape, 0)
    pos_l = q0 + lax.broadcasted_iota(I32, sc_t.shape, 1) % tq
    sc, valid = _block_scores(sc_t, pos_l, n_io)
    sel_t = _rank_select(sc, sc_scr, valid, jnp.minimum((q0 + tq - 1) // SEL_BLOCK + 1, nb))
    selb = jnp.concatenate([sel_t, jnp.zeros((LANES - nb, sc_t.shape[1]), F32)], axis=0).astype(BF16)

    m_scr[...] = jnp.full(m_scr.shape, NEG_INF, F32)
    l_scr[...] = jnp.zeros(l_scr.shape, F32)
    acc_scr[...] = jnp.zeros(acc_scr.shape, F32)
    key_i = lax.broadcasted_iota(I32, (tk, tq), 0)
    pos_q = q0 + lax.broadcasted_iota(I32, (tk, tq), 1)

    def kv_step(j, carry):
        k0 = pl.multiple_of(j * tk, tk)
        kt = sk_ref[pl.ds(k0, tk), :]
        selm = _dot(et_ref[j], selb)
        causal = k0 + key_i <= pos_q
        for g in range(NSA_GROUPS):
            bias = jnp.where((selm[:, g * tq:(g + 1) * tq] > 0.5) & causal, 0.0, NEG_INF)
            s = _dot(kt, qg[g]) + jnp.concatenate([bias] * NSA_HPG, axis=1)
            m_prev = m_scr[g]
            m_new = jnp.maximum(m_prev, jnp.max(s, axis=0, keepdims=True))
            alpha = jnp.exp(m_prev - m_new)
            p = jnp.exp(s - m_new)
            l_scr[g] = alpha * l_scr[g] + jnp.sum(p, axis=0, keepdims=True)
            vt = jnp.concatenate([svt_ref[j * (tk // LANES) + u, g * hd:(g + 1) * hd, :]
                                  for u in range(tk // LANES)], axis=1)
            acc_scr[g] = alpha * acc_scr[g] + _dot(vt, p.astype(BF16))
            m_scr[g] = m_new
        return carry

    lax.fori_loop(0, (q0 + tq + tk - 1) // tk, kv_step, 0)

    w0 = pl.multiple_of(jnp.clip(q0 + tq - wlen, 0, seq - wlen), LANES)
    kw = wk_ref[pl.ds(w0, wlen), :]
    dist = pos_t - (w0 + lax.broadcasted_iota(I32, (wlen, gl), 0))
    wmask = (dist >= 0) & (dist < WINDOW)
    o_win = []
    for g in range(NSA_GROUPS):
        e, l = _col_softmax(_dot(kw, qg[g]), wmask)
        vt = jnp.concatenate([wvt_ref[w0 // LANES + u, g * hd:(g + 1) * hd, :] for u in range(wlen // LANES)], axis=1)
        o_win.append(_dot(vt, (e * (1.0 / l)).astype(BF16)))

    ngt = ng_ref[...].T
    heads = []
    for g in range(NSA_GROUPS):
        o_sel = acc_scr[g] * (1.0 / l_scr[g])
        for r in range(NSA_HPG):
            h = NSA_HPG * g + r
            lanes = slice(r * tq, (r + 1) * tq)
            heads.append(ngt[h:h + 1, :] * o_cmp[g][:, lanes]
                         + ngt[NSA_HEADS + h:NSA_HEADS + h + 1, :] * o_sel[:, lanes]
                         + ngt[2 * NSA_HEADS + h:2 * NSA_HEADS + h + 1, :] * o_win[g][:, lanes])
    ot = jnp.concatenate(heads, axis=0)
    o_ref[...] = jnp.concatenate([ot[c * LANES:(c + 1) * LANES].T for c in range(D_NSA // LANES)],
                                 axis=1).astype(o_ref.dtype)


def _nsa_prompt(nqt, ng, ck, cvt, skb, svt, wkb, wvt, *, batch, seq):
    tq, tk, wlen = LANES, 256, 640
    assert seq % tk == 0 and seq >= wlen and wlen >= WINDOW + tq
    nseg = seq // CMP_STRIDE
    n_blocks = seq // SEL_BLOCK
    nq_t = seq // tq
    ov, et, _ = _nsa_tables(nseg, nseg - 1, n_blocks, n_blocks, seq, tk, key_major=True)
    gl = NSA_HPG * tq
    qrow = lambda b, i: (b * nq_t + i, 0)
    full = lambda a: pl.BlockSpec(a.shape, lambda b, i: (0,) * a.ndim)
    kspec = pl.BlockSpec((seq, D_GK), lambda b, i: (b, 0))
    vtspec = pl.BlockSpec((seq // LANES, D_GK, LANES), lambda b, i: (b, 0, 0))
    return pl.pallas_call(
        functools.partial(_nsa_prompt_kernel, tq=tq, tk=tk, wlen=wlen),
        grid=(batch, nq_t),
        in_specs=[pl.BlockSpec((1, D_NSA, tq), lambda b, i: (b * nq_t + i, 0, 0)), pl.BlockSpec((tq, LANES), qrow),
                  pl.BlockSpec((nseg, D_GK), lambda b, i: (b, 0)),
                  pl.BlockSpec((None, D_GK, nseg), lambda b, i: (b, 0, 0)),
                  kspec, vtspec, kspec, vtspec, full(ov), full(et)],
        out_specs=pl.BlockSpec((tq, D_NSA), qrow),
        out_shape=jax.ShapeDtypeStruct((batch * seq, D_NSA), BF16),
        scratch_shapes=[pltpu.VMEM((NSA_GROUPS, 1, gl), F32), pltpu.VMEM((NSA_GROUPS, 1, gl), F32),
                        pltpu.VMEM((NSA_GROUPS, NSA_HEAD_DIM, gl), F32),
                        pltpu.VMEM((n_blocks, NSA_GROUPS * tq), F32)],
        compiler_params=_params("arbitrary", "arbitrary"),
        name="nsa_prompt",
    )(nqt, ng, ck, cvt, skb, svt, wkb, wvt, ov, et)


def _nsa_sample_kernel(pt_ref, *refs, n_tok, ppb, n_blocks, past):
    cmp_refs = refs[:ppb]
    sel_refs = refs[ppb:2 * ppb]
    (q_ref, ng_ref, snew_ref, wnew_ref, wint_ref, perm_ref, w1c_ref, w2_ref, w2t_ref, pe8_ref, w1f_ref, ov_ref,
     e_ref, eg_ref, o_ref, wout_ref, lohi_scr, sel_scr, s_scr, pe_scr, sc_scr) = refs[2 * ppb:]
    del pt_ref
    b = pl.program_id(0)
    st = pl.program_id(1)
    n_st = sel_scr.shape[0]
    nseg = lohi_scr.shape[2]
    tq = n_tok
    rows = NSA_HEADS * tq
    tkp = ppb * PAGE_SIZE

    @pl.when((b == 0) & (st == 0))
    def _():
        for c in range(2):
            pe_scr[c] = jnp.broadcast_to(_cmp_pe_term(pe8_ref, w1f_ref, c), (8, CMP_HID))

    _cmp_store([r[...] for r in cmp_refs], perm_ref[...], w1c_ref, lohi_scr, st, feature_major=True)
    for k, r in enumerate(sel_refs):
        sel_scr[st, :, k * PAGE_SIZE:(k + 1) * PAGE_SIZE] = r[...].astype(BF16)

    @pl.when(st == n_st - 1)
    def _():
        qbd = _q_blockdiag(q_ref[...], tq)
        pos_r = past + lax.broadcasted_iota(I32, (rows, 1), 0) % tq
        pad = jnp.zeros((LANES - tq, D_KV), F32)
        snew_t = jnp.concatenate([snew_ref[...], pad], axis=0).T.astype(BF16)
        wnew_t = jnp.concatenate([wnew_ref[...], pad], axis=0).T

        kct = _cmp_finish(lambda g: lohi_scr[0, g], pe_scr[0, 0:1], w2t_ref[0], True).astype(BF16)
        vc = _cmp_finish(lambda g: lohi_scr[1, g], pe_scr[1, 0:1], w2_ref[1], False).astype(BF16)
        s = _dot(qbd, kct)
        cidx = lax.broadcasted_iota(I32, (rows, nseg), 1)
        cmask = (CMP_STRIDE * cidx + (CMP_LEN - 1) <= pos_r) & (cidx < nseg - 1)
        p = jnp.where(cmask, _masked_softmax(s, cmask), 0.0)
        o_cmp = _pair_pv(p.astype(BF16), vc)
        pg = jnp.sum(p.reshape(NSA_GROUPS, NSA_HPG, tq, nseg), axis=1, keepdims=True)
        pg = jnp.broadcast_to(pg, (NSA_GROUPS, NSA_HPG, tq, nseg)).reshape(rows, nseg)
        ph, plo = _split(pg)
        sc_t = _dot_nt(ov_ref[...], ph) + _dot_nt(ov_ref[...], plo)

        nb_pad = ov_ref.shape[0]
        n_io = lax.broadcasted_iota(I32, (nb_pad, rows), 0)
        pos_l = past + lax.broadcasted_iota(I32, (nb_pad, rows), 1) % tq
        sc, valid = _block_scores(sc_t, pos_l, n_io)
        sel = _rank_select(sc, sc_scr, valid & (n_io < n_blocks), n_blocks).T.astype(BF16)

        for t in range(n_st):
            selm = _dot(sel, e_ref[t])
            s_scr[:, t * tkp:(t + 1) * tkp] = jnp.where(selm > 0.5, _dot(qbd, sel_scr[t, 0:D_GK, :]), NEG_INF)
        selm = _dot(sel, e_ref[n_st, :, 0:LANES])
        new_ok = (selm > 0.5) & (past + lax.broadcasted_iota(I32, (rows, LANES), 1) <= pos_r)
        s_scr[:, n_st * tkp:] = jnp.where(new_ok, _dot(qbd, snew_t[0:D_GK]), NEG_INF)
        s_all = s_scr[...]
        ps = jnp.exp(s_all - jnp.max(s_all, axis=-1, keepdims=True))
        l_sel = jnp.sum(ps, axis=-1, keepdims=True)
        ps = ps.astype(BF16)
        acc = _dot_nt(ps[:, n_st * tkp:], snew_t[D_GK:])
        for t in range(n_st):
            acc = acc + _dot_nt(ps[:, t * tkp:(t + 1) * tkp], sel_scr[t, D_GK:, :])
        acc = acc * (1.0 / l_sel)
        half = rows // 2
        o_sel = jnp.concatenate([acc[:half, :LANES], acc[half:, LANES:]], axis=0)

        win_buf = wint_ref.shape[1]
        wt = wint_ref[...]
        wtb = wt.astype(BF16)
        wnb = wnew_t.astype(BF16)
        sw = jnp.concatenate([_dot(qbd, wtb[0:D_GK]), _dot(qbd, wnb[0:D_GK])], axis=1)
        k_pos = past - win_buf + lax.broadcasted_iota(I32, sw.shape, 1)
        dist = pos_r - k_pos
        pw = _masked_softmax(sw, (dist >= 0) & (dist < WINDOW) & (k_pos >= 0)).astype(BF16)
        accw = _dot_nt(pw[:, :win_buf], wtb[D_GK:]) + _dot_nt(pw[:, win_buf:], wnb[D_GK:])
        o_win = jnp.concatenate([accw[:half, :LANES], accw[half:, LANES:]], axis=0)

        o_ref[...] = _gate_branches(ng_ref[...], eg_ref, (o_cmp, o_sel, o_win), tq)

        shifted = pltpu.roll(wt, win_buf - tq, 1)
        newr = pltpu.roll(wnew_t, LANES - tq, 1)
        lane = lax.broadcasted_iota(I32, (D_KV, LANES), 1)
        wout_ref[:, 0:win_buf - LANES] = shifted[:, 0:win_buf - LANES]
        wout_ref[:, win_buf - LANES:] = jnp.where(lane >= LANES - tq, newr, shifted[:, win_buf - LANES:])


def _nsa_sample(nq, ng, skv, wkv, cache_cmp_t, cache_sel_t, cache_win_t, page_table, cmp_w, *, n_tok):
    db, n_pages = page_table.shape
    past = n_pages * PAGE_SIZE
    win_buf = cache_win_t.shape[2]
    ppb = 8
    tkp = ppb * PAGE_SIZE
    assert n_tok == 8 and n_tok < CMP_STRIDE and n_pages % ppb == 0 and win_buf % LANES == 0 and win_buf >= LANES
    n_st = n_pages // ppb
    nseg = past // CMP_STRIDE
    n_blocks = -(-(past + n_tok) // SEL_BLOCK)
    nb_pad = -(-n_blocks // LANES) * LANES
    ov, e3, eg = _nsa_tables(nseg, nseg - 1, n_blocks, nb_pad, (n_st + 1) * tkp, tkp)
    rows = NSA_HEADS * n_tok

    def page_spec(k):
        return pl.BlockSpec((None, D_KV, PAGE_SIZE), lambda b, s, pt: (pt[b, s * ppb + k], 0, 0))

    req = lambda cols: pl.BlockSpec((n_tok, cols), lambda b, s, pt: (b, 0))
    full = lambda a: pl.BlockSpec(a.shape, lambda b, s, pt: (0,) * a.ndim)
    win_spec = pl.BlockSpec((None, D_KV, win_buf), lambda b, s, pt: (b, 0, 0))
    grid_spec = pltpu.PrefetchScalarGridSpec(
        num_scalar_prefetch=1,
        grid=(db, n_st),
        in_specs=[page_spec(k) for k in range(ppb)] + [page_spec(k) for k in range(ppb)]
        + [req(D_NSA), req(LANES), req(D_KV), req(D_KV), win_spec]
        + [full(a) for a in cmp_w]
        + [full(ov), pl.BlockSpec(e3.shape, lambda b, s, pt: (0, 0, 0), pipeline_mode=pl.Buffered(1)), full(eg)],
        out_specs=[req(D_NSA), win_spec],
        scratch_shapes=[pltpu.VMEM((2, NSA_GROUPS, nseg, 2 * CMP_HID), F32),
                        pltpu.VMEM((n_st, D_KV, tkp), BF16),
                        pltpu.VMEM((rows, n_st * tkp + LANES), F32),
                        pltpu.VMEM((2, 8, CMP_HID), F32),
                        pltpu.VMEM((nb_pad, rows), F32)],
    )
    return pl.pallas_call(
        functools.partial(_nsa_sample_kernel, n_tok=n_tok, ppb=ppb, n_blocks=n_blocks, past=past),
        grid_spec=grid_spec,
        out_shape=[jax.ShapeDtypeStruct((db * n_tok, D_NSA), F32),
                   jax.ShapeDtypeStruct((db, D_KV, win_buf), F32)],
        compiler_params=_params("arbitrary", "arbitrary"),
        name="nsa_sample",
    )(page_table, *([cache_cmp_t] * ppb), *([cache_sel_t] * ppb), nq, ng, skv, wkv, cache_win_t,
      *cmp_w, ov, e3, eg)


def _route(lg):
    lane = lax.broadcasted_iota(I32, lg.shape, 1).astype(F32)
    big = float(1 << 20)
    is_g = lane < N_GROUPS
    gl = jnp.where(is_g, lg, -jnp.inf)
    gmax = jnp.max(gl, axis=-1, keepdims=True)
    g_star = jnp.min(jnp.where(gl == gmax, lane, big), axis=-1, keepdims=True)
    g_w = 1.0 / jnp.sum(jnp.where(is_g, jnp.exp(lg - gmax), 0.0), axis=-1, keepdims=True)
    lo = N_GROUPS + g_star * EXPERTS_PER_GROUP
    el = jnp.where((lane >= lo) & (lane < lo + EXPERTS_PER_GROUP), lg, -jnp.inf)
    m1 = jnp.max(el, axis=-1, keepdims=True)
    i1 = jnp.min(jnp.where(el == m1, lane, big), axis=-1, keepdims=True)
    el2 = jnp.where(lane == i1, -jnp.inf, el)
    m2 = jnp.max(el2, axis=-1, keepdims=True)
    i2 = jnp.min(jnp.where(el2 == m2, lane, big), axis=-1, keepdims=True)
    e2 = jnp.exp(m2 - m1)
    p1 = 1.0 / (1.0 + e2)
    out = jnp.where(lane == 0, i1 - N_GROUPS, 0.0)
    out = jnp.where(lane == 1, i2 - N_GROUPS, out)
    out = jnp.where(lane == 2, g_w * p1, out)
    return jnp.where(lane == 3, g_w * (e2 * p1), out)


def _mixout_kernel(x_ref, yr_ref, on_ref, mg_ref, gt_ref, sh_ref, sc_ref, g2_ref, wr_ref, wn_ref, wo_ref,
                   wrt_ref, brt_ref, x1_ref, h2_ref, rt_ref):
    y_ret = _dot(yr_ref[...], wr_ref[...])
    y_nsa = _dot(on_ref[...].astype(BF16), wn_ref[...])
    mg = mg_ref[...].astype(F32)
    z = mg[:, :D_MODEL] * y_ret + mg[:, D_MODEL:] * y_nsa
    x1 = x_ref[...] + gt_ref[...] * _dot(z.astype(BF16), wo_ref[...])
    x1_ref[...] = x1
    h = x1 * lax.rsqrt(jnp.mean(x1 * x1, axis=-1, keepdims=True) + RMS_EPS) * g2_ref[...]
    h2 = h * (1.0 + sc_ref[...]) + sh_ref[...]
    h2_ref[...] = h2
    rt_ref[...] = _route(_dot3(h2, wrt_ref[...]) + brt_ref[...])


def _mixout(x, yret, onsa, mg, gate, shift, scale, g2, w_ret_o, w_nsa_o, w_out, w_rt, b_rt, *, tm, rows_per_mod):
    t, d = x.shape
    r = gate.shape[1]
    row = lambda cols: pl.BlockSpec((tm, cols), lambda i: (i, 0))
    mod_spec = pl.BlockSpec((None, r, d), lambda i: (i // rows_per_mod, 0, 0))
    full = lambda a: pl.BlockSpec(a.shape, lambda i: (0,) * a.ndim)
    return pl.pallas_call(
        _mixout_kernel,
        grid=(t // tm,),
        in_specs=[row(d), row(D_RET_V), row(D_NSA), row(2 * d), mod_spec, mod_spec, mod_spec,
                  pl.BlockSpec((1, d), lambda i: (0, 0)), full(w_ret_o), full(w_nsa_o), full(w_out),
                  full(w_rt), full(b_rt)],
        out_specs=[row(d), row(d), row(LANES)],
        out_shape=[jax.ShapeDtypeStruct((t, d), F32), jax.ShapeDtypeStruct((t, d), F32),
                   jax.ShapeDtypeStruct((t, LANES), F32)],
        compiler_params=_params("arbitrary"),
        name="mixout",
    )(x, yret, onsa, mg, gate, shift, scale, g2.reshape(1, d), w_ret_o, w_nsa_o, w_out, w_rt, b_rt)


def _moe_kernel(be_ref, rt_ref, nu_ref, x_hbm, w1_ref, w3_ref, w2_ref, y_ref, xbuf, sem, *, bm):
    i = pl.program_id(0)
    n_used = nu_ref[0]

    def row_copy(blk, slot, r):
        tok = rt_ref[blk * bm + r]
        return pltpu.make_async_copy(x_hbm.at[pl.ds(tok, 1), :], xbuf.at[slot, pl.ds(r, 1), :], sem.at[slot])

    def issue(blk, slot):
        def body(r, c):
            row_copy(blk, slot, r).start()
            return c
        lax.fori_loop(0, bm, body, 0)

    @pl.when((i == 0) & (n_used > 0))
    def _():
        issue(0, 0)

    @pl.when(i + 1 < n_used)
    def _():
        issue(i + 1, (i + 1) % 2)

    @pl.when(i < n_used)
    def _():
        slot = i % 2
        pltpu.make_async_copy(x_hbm.at[pl.ds(0, bm), :], xbuf.at[slot], sem.at[slot]).wait()
        xb = xbuf[slot].astype(BF16)
        hid = _silu(_dot(xb, w1_ref[...])) * _dot(xb, w3_ref[...])
        y_ref[...] = _dot(hid.astype(BF16), w2_ref[...])

    @pl.when(i >= n_used)
    def _():
        y_ref[...] = jnp.zeros(y_ref.shape, F32)


def _moe_experts(h2, blk_expert, row_tok, n_used, w_e1, w_e3, w_e2, *, bm):
    t, d = h2.shape
    n_blk = blk_expert.shape[0]
    de = w_e1.shape[-1]
    grid_spec = pltpu.PrefetchScalarGridSpec(
        num_scalar_prefetch=3,
        grid=(n_blk,),
        in_specs=[pl.BlockSpec(memory_space=pl.ANY),
                  pl.BlockSpec((None, d, de), lambda i, be, rt, nu: (be[i], 0, 0)),
                  pl.BlockSpec((None, d, de), lambda i, be, rt, nu: (be[i], 0, 0)),
                  pl.BlockSpec((None, de, d), lambda i, be, rt, nu: (be[i], 0, 0))],
        out_specs=pl.BlockSpec((bm, d), lambda i, be, rt, nu: (i, 0)),
        scratch_shapes=[pltpu.VMEM((2, bm, d), F32), pltpu.SemaphoreType.DMA((2,))],
    )
    return pl.pallas_call(
        functools.partial(_moe_kernel, bm=bm),
        grid_spec=grid_spec,
        out_shape=jax.ShapeDtypeStruct((n_blk * bm, d), F32),
        compiler_params=_params("arbitrary"),
        name="moe_experts",
    )(blk_expert, row_tok, n_used, h2, w_e1, w_e3, w_e2)


def _dispatch_plan(expert, *, bm):
    t = expert.shape[0]
    tk = t * EXPERT_TOPK
    e_flat = expert.reshape(-1)
    onehot = (e_flat[:, None] == jnp.arange(N_EXPERTS, dtype=I32)[None, :]).astype(I32)
    csum = jnp.cumsum(onehot, axis=0)
    counts = csum[-1]
    rank = jnp.sum((csum - onehot) * onehot, axis=1)
    padded = (counts + bm - 1) // bm * bm
    pad_end = jnp.cumsum(padded)
    pad_start = pad_end - padded
    dest = pad_start[e_flat] + rank
    n_blk = -(-tk // bm) + N_EXPERTS
    blk_expert = jnp.minimum(jnp.searchsorted(pad_end, jnp.arange(n_blk, dtype=I32) * bm, side='right'),
                             N_EXPERTS - 1).astype(I32)
    row_tok = jnp.zeros((n_blk * bm,), I32).at[dest].set(jnp.arange(tk, dtype=I32) // EXPERT_TOPK)
    n_used = (pad_end[-1:] // bm).astype(I32)
    return dest.reshape(t, EXPERT_TOPK), blk_expert, row_tok, n_used


def _final_kernel(ds_ref, x1_ref, rt_ref, gt_ref, gf_ref, y_hbm, o_ref, ybuf, sem, *, tm, row0):
    i = pl.program_id(0)
    n = pl.num_programs(0)

    def issue(blk, slot):
        def body(r, c):
            for k in range(EXPERT_TOPK):
                src = ds_ref[(row0 + blk * tm + r) * EXPERT_TOPK + k]
                pltpu.make_async_copy(y_hbm.at[pl.ds(src, 1), :], ybuf.at[slot, k, pl.ds(r, 1), :],
                                      sem.at[slot]).start()
            return c
        lax.fori_loop(0, tm, body, 0)

    @pl.when(i == 0)
    def _():
        issue(0, 0)

    @pl.when(i + 1 < n)
    def _():
        issue(i + 1, (i + 1) % 2)

    slot = i % 2
    for k in range(EXPERT_TOPK):
        pltpu.make_async_copy(y_hbm.at[pl.ds(0, tm), :], ybuf.at[slot, k], sem.at[slot]).wait()
    rt = rt_ref[...]
    moe = rt[:, 2:3] * ybuf[slot, 0] + rt[:, 3:4] * ybuf[slot, 1]
    x2 = x1_ref[...] + gt_ref[...] * moe
    o_ref[...] = x2 * lax.rsqrt(jnp.mean(x2 * x2, axis=-1, keepdims=True) + RMS_EPS) * gf_ref[...]


def _final(dest_flat, x1, route, gate, g_final, y_buf, *, tm, rows_per_mod, row0):
    t, d = x1.shape
    r = gate.shape[1]
    grid_spec = pltpu.PrefetchScalarGridSpec(
        num_scalar_prefetch=1,
        grid=(t // tm,),
        in_specs=[pl.BlockSpec((tm, d), lambda i, ds: (i, 0)), pl.BlockSpec((tm, LANES), lambda i, ds: (i, 0)),
                  pl.BlockSpec((None, r, d), lambda i, ds: (i // rows_per_mod, 0, 0)),
                  pl.BlockSpec((1, d), lambda i, ds: (0, 0)), pl.BlockSpec(memory_space=pl.ANY)],
        out_specs=pl.BlockSpec((tm, d), lambda i, ds: (i, 0)),
        scratch_shapes=[pltpu.VMEM((2, EXPERT_TOPK, tm, d), F32), pltpu.SemaphoreType.DMA((2,))],
    )
    return pl.pallas_call(
        functools.partial(_final_kernel, tm=tm, row0=row0),
        grid_spec=grid_spec,
        out_shape=jax.ShapeDtypeStruct((t, d), F32),
        compiler_params=_params("arbitrary"),
        name="final",
    )(dest_flat, x1, route, gate, g_final.reshape(1, d), y_buf)


def kernel(x_prompt, x_sample, c_prompt, c_sample, state_ret, cache_cmp_kv, cache_sel_kv, cache_win_kv, page_table,
           w_ada, b_ada, g_norm1, g_norm2, g_final, w_in, ret_norm_g, cmp_w1, cmp_pe, cmp_w2, w_ret_o, w_nsa_o,
           w_out, w_grp, b_grp, w_exp, b_exp, w_e1, w_e3, w_e2):
    depth = w_in.shape[0]
    assert depth == 1, "one layer"
    batch, seq, d = x_prompt.shape
    db, n_tok, _ = x_sample.shape
    n_pages = page_table.shape[1]
    past = n_pages * PAGE_SIZE
    win_buf = cache_win_kv.shape[2]
    assert win_buf <= seq
    tp = batch * seq
    ts = db * n_tok
    l = 0

    mod = _adaln(jnp.concatenate([c_prompt, c_sample], axis=0).astype(F32), w_ada[l], b_ada[l])
    mods = jnp.split(mod, 6, axis=-1)
    mod_p = [m[:batch].reshape(batch, 1, d) for m in mods]
    ts_tile = min(ts, 256)
    mod_s = [jnp.repeat(m[batch:], n_tok, axis=0).reshape(ts // ts_tile, ts_tile, d) for m in mods]

    w_pad = _pad_w_in(w_in[l])
    w_t = jnp.concatenate([w_in[l][:, C_NQ:C_CKV], w_in[l][:, C_SKV + D_GK:C_WKV], w_in[l][:, C_WKV + D_GK:C_NG]],
                          axis=1).T.astype(BF16)
    cmp_w = _cmp_weights(cmp_w1[l], cmp_pe[l], cmp_w2[l])
    wr, wn, wo = w_ret_o[l].astype(BF16), w_nsa_o[l].astype(BF16), w_out[l].astype(BF16)
    w_rt = jnp.concatenate([w_grp[l], w_exp[l], jnp.zeros((d, LANES - N_GROUPS - N_EXPERTS), F32)], axis=1)
    b_rt = jnp.concatenate([b_grp[l], b_exp[l], jnp.zeros((LANES - N_GROUPS - N_EXPERTS,), F32)]).reshape(1, LANES)

    tm_p = 256
    cos_p, sin_p = _rope_tables(jnp.arange(seq))
    (rq, rk, rv, rg, nqt, ckv_p, skv_p, wkv_p, skb, wkb, ng, mg, svt, wvt) = _inproj(
        x_prompt.reshape(tp, d).astype(F32), mod_p[0], mod_p[1], g_norm1[l], cos_p, sin_p, w_pad, w_t,
        tm=tm_p, rows_per_mod=seq // tm_p, pos_blocks=seq // tm_p, act_dtype=BF16)
    yret_p, ret_state_p = _ret_prompt(rq, rk, rv, rg, ret_norm_g[l], batch=batch, seq=seq)
    ck, cvt = _cmp_prompt(ckv_p, cmp_w, batch=batch, seq=seq)
    onsa_p = _nsa_prompt(nqt, ng, ck, cvt, skb, svt, wkb, wvt, batch=batch, seq=seq)
    tm_m = 512
    x1_p, h2_p, route_p = _mixout(x_prompt.reshape(tp, d).astype(F32), yret_p, onsa_p, mg, mod_p[2], mod_p[3],
                                  mod_p[4], g_norm2[l], wr, wn, wo, w_rt, b_rt, tm=tm_m, rows_per_mod=seq // tm_m)

    cos_s, sin_s = _rope_tables(past + jnp.arange(n_tok))
    cos_s = jnp.tile(cos_s, (ts_tile // n_tok, 1))
    sin_s = jnp.tile(sin_s, (ts_tile // n_tok, 1))
    (rq_s, rk_s, rv_s, rg_s, nq_s, ckv_s, skv_s, wkv_s, _, _, ng_s, mg_s) = _inproj(
        x_sample.reshape(ts, d).astype(F32), mod_s[0], mod_s[1], g_norm1[l], cos_s, sin_s, w_pad, None,
        tm=ts_tile, rows_per_mod=1, pos_blocks=1, act_dtype=F32)
    yret_s, ret_state_s = _ret_sample(rq_s, rk_s, rv_s, rg_s, state_ret[l].astype(F32), ret_norm_g[l], n_tok=n_tok)

    def feature_major(cache):
        n, r = cache.shape[:2]
        return jnp.transpose(cache, (0, 2, 3, 4, 1)).reshape(n, D_KV, r).astype(F32)

    onsa_s, win_s_t = _nsa_sample(nq_s, ng_s, skv_s, wkv_s, feature_major(cache_cmp_kv[l]),
                                  feature_major(cache_sel_kv[l]), feature_major(cache_win_kv[l]), page_table, cmp_w,
                                  n_tok=n_tok)
    win_s = jnp.transpose(win_s_t.reshape(db, 2, NSA_GROUPS, NSA_HEAD_DIM, win_buf), (0, 4, 1, 2, 3))
    x1_s, h2_s, route_s = _mixout(x_sample.reshape(ts, d).astype(F32), yret_s, onsa_s, mg_s, mod_s[2], mod_s[3],
                                  mod_s[4], g_norm2[l], wr, wn, wo, w_rt, b_rt, tm=ts_tile, rows_per_mod=1)

    bm = 256
    h2 = jnp.concatenate([h2_p, h2_s], axis=0)
    route = jnp.concatenate([route_p, route_s], axis=0)
    expert = route[:, :EXPERT_TOPK].astype(I32)
    dest, blk_expert, row_tok, n_used = _dispatch_plan(expert, bm=bm)
    y_buf = _moe_experts(h2, blk_expert, row_tok, n_used,
                         w_e1[l].astype(BF16), w_e3[l].astype(BF16), w_e2[l].astype(BF16), bm=bm)
    dest_flat = dest.reshape(-1)
    y_p = _final(dest_flat, x1_p, route_p, mod_p[5], g_final, y_buf, tm=tm_p, rows_per_mod=seq // tm_p, row0=0)
    y_s = _final(dest_flat, x1_s, route_s, mod_s[5], g_final, y_buf, tm=ts_tile, rows_per_mod=1, row0=tp)

    kv6 = lambda a, n, s: a.reshape(1, n, s, 2, NSA_GROUPS, NSA_HEAD_DIM)
    win_p = wkv_p.reshape(batch, seq, D_KV)[:, seq - win_buf:]
    return (y_p.reshape(batch, seq, d).astype(x_prompt.dtype), y_s.reshape(db, n_tok, d).astype(x_sample.dtype),
            ret_state_p[None].astype(state_ret.dtype), ret_state_s[None].astype(state_ret.dtype),
            kv6(ckv_p, batch, seq).astype(cache_cmp_kv.dtype), kv6(ckv_s, db, n_tok).astype(cache_cmp_kv.dtype),
            kv6(skv_p, batch, seq).astype(cache_sel_kv.dtype), kv6(skv_s, db, n_tok).astype(cache_sel_kv.dtype),
            kv6(win_p, batch, win_buf).astype(cache_win_kv.dtype), kv6(win_s, db, win_buf).astype(cache_win_kv.dtype))
```

```python
import functools

import numpy as np
import jax
import jax.numpy as jnp
from jax import lax
from jax.experimental import pallas as pl
from jax.experimental.pallas import tpu as pltpu

F32 = jnp.float32
BF16 = jnp.bfloat16
I32 = jnp.int32

D_MODEL = 1024
PAGE_SIZE = 128
RET_HEADS = 8
RET_DK = 64
RET_DV = 128
ROPE_BASE = 10000.0
NSA_HEADS = 16
NSA_GROUPS = 4
NSA_HPG = NSA_HEADS // NSA_GROUPS
NSA_HEAD_DIM = 64
CMP_LEN = 32
CMP_STRIDE = 16
CMP_HID = 128
SEL_BLOCK = 64
SEL_TOPK = 16
WINDOW = 512
N_GROUPS = 4
EXPERTS_PER_GROUP = 8
N_EXPERTS = N_GROUPS * EXPERTS_PER_GROUP
EXPERT_TOPK = 2
D_EXPERT = 512
RMS_EPS = 1e-6
LOG2E = 1.4426950408889634
NEG_INF = -1e30
FORCE_SCORE = 1e9

LANES = 128
ONES_ROWS = 16
D_RET_QK = RET_HEADS * RET_DK
D_RET_V = RET_HEADS * RET_DV
D_NSA = NSA_HEADS * NSA_HEAD_DIM
D_KV = 2 * NSA_GROUPS * NSA_HEAD_DIM
D_GK = NSA_GROUPS * NSA_HEAD_DIM
N_GATE = 3 * NSA_HEADS
C_RQ = 0
C_RK = C_RQ + D_RET_QK
C_RV = C_RK + D_RET_QK
C_RG = C_RV + D_RET_V
C_NQ = C_RG + D_RET_V
C_CKV = C_NQ + D_NSA
C_SKV = C_CKV + D_KV
C_WKV = C_SKV + D_KV
C_NG = C_WKV + D_KV
C_MG = C_NG + LANES
C_END = C_MG + 2 * D_MODEL
VMEM_LIMIT = 56 << 20


def _dot(a, b):
    return jnp.dot(a, b, preferred_element_type=F32)


def _dot_nt(a, b):
    return lax.dot_general(a, b, (((1,), (1,)), ((), ())), preferred_element_type=F32)


def _dot_tn(a, b):
    return lax.dot_general(a, b, (((0,), (0,)), ((), ())), preferred_element_type=F32)


def _split(x):
    hi = x.astype(BF16)
    return hi, (x - hi.astype(F32)).astype(BF16)


def _dot3(a, w):
    ah, al = _split(a)
    wh, wl = _split(w)
    return _dot(ah, wh) + _dot(ah, wl) + _dot(al, wh)


def _silu(x):
    return x * jax.nn.sigmoid(x)


def _params(*sem):
    return pltpu.CompilerParams(dimension_semantics=sem, vmem_limit_bytes=VMEM_LIMIT)


def _adaln_kernel(c_ref, w_ref, b_ref, o_ref):
    o_ref[...] = _dot3(_silu(c_ref[...]), w_ref[...]) + b_ref[...]


def _adaln(c, w, b):
    r, d = c.shape
    n = w.shape[1]
    tn = 1024
    return pl.pallas_call(
        _adaln_kernel,
        grid=(n // tn,),
        in_specs=[pl.BlockSpec((r, d), lambda j: (0, 0)),
                  pl.BlockSpec((d, tn), lambda j: (0, j)),
                  pl.BlockSpec((1, tn), lambda j: (0, j))],
        out_specs=pl.BlockSpec((r, tn), lambda j: (0, j)),
        out_shape=jax.ShapeDtypeStruct((r, n), F32),
        compiler_params=_params("arbitrary"),
        name="adaln",
    )(c, w, b.reshape(1, n))


def _rope(x, cos, sin):
    outs = []
    lane = lax.broadcasted_iota(I32, cos.shape, 1)
    first = (lane & 32) == 0
    for j in range(x.shape[1] // LANES):
        xj = x[:, j * LANES:(j + 1) * LANES]
        partner = jnp.where(first, pltpu.roll(xj, LANES - 32, 1), pltpu.roll(xj, 32, 1))
        outs.append(xj * cos + partner * sin)
    return jnp.concatenate(outs, axis=1)


def _inproj_kernel(x_ref, sh_ref, sc_ref, g_ref, cos_ref, sin_ref, w_ref, *rest, key_major):
    if key_major:
        (wt_ref, rq_ref, rk_ref, rv_ref, rg_ref, nq_ref, ckv_ref, skv_ref, wkv_ref, skb_ref, wkb_ref, ng_ref,
         mg_ref, svt_ref, wvt_ref) = rest
    else:
        (rq_ref, rk_ref, rv_ref, rg_ref, nq_ref, ckv_ref, skv_ref, wkv_ref, ng_ref, mg_ref) = rest
    x = x_ref[...]
    h = x * lax.rsqrt(jnp.mean(x * x, axis=-1, keepdims=True) + RMS_EPS) * g_ref[...]
    hb = (h * (1.0 + sc_ref[...]) + sh_ref[...]).astype(BF16)

    def proj(a, b):
        return _dot(hb, w_ref[:, a:b])

    cos = cos_ref[...]
    sin = sin_ref[...]
    rq_ref[...] = _rope(proj(C_RQ, C_RK), cos, sin).astype(rq_ref.dtype)
    rk_ref[...] = (_rope(proj(C_RK, C_RV), cos, sin) * (RET_DK ** -0.5)).astype(rk_ref.dtype)
    rv_ref[...] = proj(C_RV, C_RG).astype(rv_ref.dtype)
    rg_ref[...] = proj(C_RG, C_NQ).astype(rg_ref.dtype)
    if key_major:
        n_t = nq_ref.shape[0]

        def proj_t(a, b):
            return _dot_nt(wt_ref[a:b, :], hb)

        def tiles(o_ref, y):
            for k in range(n_t):
                o_ref[k] = y[:, k * LANES:(k + 1) * LANES].astype(o_ref.dtype)

        tiles(nq_ref, proj_t(0, D_NSA) * (NSA_HEAD_DIM ** -0.5 * LOG2E))
        ckv_ref[...] = proj_t(D_NSA, D_NSA + D_KV)
        skv = proj_t(D_NSA + D_KV, D_NSA + 2 * D_KV)
        skv_ref[...] = skv
        tiles(svt_ref, skv[D_GK:])
        wkv = proj_t(D_NSA + 2 * D_KV, D_NSA + 3 * D_KV)
        wkv_ref[...] = wkv
        tiles(wvt_ref, wkv[D_GK:])
        skb_ref[...] = proj(C_SKV, C_SKV + D_GK).astype(BF16)
        wkb_ref[...] = proj(C_WKV, C_WKV + D_GK).astype(BF16)
    else:
        nq_ref[...] = (proj(C_NQ, C_CKV) * (NSA_HEAD_DIM ** -0.5)).astype(nq_ref.dtype)
        ckv_ref[...] = proj(C_CKV, C_SKV)
        skv_ref[...] = proj(C_SKV, C_WKV)
        wkv_ref[...] = proj(C_WKV, C_NG)
    ng_ref[...] = jax.nn.sigmoid(proj(C_NG, C_MG))
    mg_ref[...] = jax.nn.sigmoid(proj(C_MG, C_END)).astype(BF16)


def _inproj(x, shift, scale, g, cos, sin, w_pad, w_t, *, tm, rows_per_mod, pos_blocks, act_dtype):
    t, d = x.shape
    r = shift.shape[1]
    key_major = w_t is not None
    mod_spec = pl.BlockSpec((None, r, d), lambda i: (i // rows_per_mod, 0, 0))
    pos_spec = pl.BlockSpec((tm, LANES), lambda i: (i % pos_blocks, 0))

    def out(cols, dt):
        return pl.BlockSpec((tm, cols), lambda i: (i, 0)), jax.ShapeDtypeStruct((t, cols), dt)

    def out_t(rows):
        return (pl.BlockSpec((tm // LANES, rows, LANES), lambda i: (i, 0, 0)),
                jax.ShapeDtypeStruct((t // LANES, rows, LANES), BF16))

    def out_fm():
        return (pl.BlockSpec((None, D_KV, tm), lambda i: (i // rows_per_mod, 0, i % rows_per_mod)),
                jax.ShapeDtypeStruct((t // (tm * rows_per_mod), D_KV, tm * rows_per_mod), F32))

    outs = [out(D_RET_QK, act_dtype), out(D_RET_QK, act_dtype), out(D_RET_V, act_dtype), out(D_RET_V, act_dtype)]
    in_specs = [pl.BlockSpec((tm, d), lambda i: (i, 0)), mod_spec, mod_spec,
                pl.BlockSpec((1, d), lambda i: (0, 0)), pos_spec, pos_spec,
                pl.BlockSpec((d, C_END), lambda i: (0, 0), pipeline_mode=pl.Buffered(1))]
    args = [x, shift, scale, g.reshape(1, d), cos, sin, w_pad]
    if key_major:
        outs += [out_t(D_NSA), out_fm(), out_fm(), out_fm(), out(D_GK, BF16), out(D_GK, BF16),
                 out(LANES, F32), out(2 * D_MODEL, BF16), out_t(D_GK), out_t(D_GK)]
        in_specs.append(pl.BlockSpec(w_t.shape, lambda i: (0, 0), pipeline_mode=pl.Buffered(1)))
        args.append(w_t)
    else:
        outs += [out(D_NSA, act_dtype), out(D_KV, F32), out(D_KV, F32), out(D_KV, F32),
                 out(LANES, F32), out(2 * D_MODEL, BF16)]
    return pl.pallas_call(
        functools.partial(_inproj_kernel, key_major=key_major),
        grid=(t // tm,),
        in_specs=in_specs,
        out_specs=[o[0] for o in outs],
        out_shape=[o[1] for o in outs],
        compiler_params=_params("arbitrary"),
        name="inproj",
    )(*args)


def _rope_tables(pos):
    half = RET_DK // 2
    inv_freq = ROPE_BASE ** (-jnp.arange(half, dtype=F32) / half)
    ang = pos.astype(F32)[:, None] * inv_freq[None, :]
    cos, sin = jnp.cos(ang), jnp.sin(ang)
    return jnp.tile(cos, (1, 4)), jnp.tile(jnp.concatenate([-sin, sin], axis=1), (1, 2))


def _pad_w_in(w_in):
    d = w_in.shape[0]
    n_real = C_NG + N_GATE
    return jnp.concatenate([w_in[:, :n_real], jnp.zeros((d, LANES - N_GATE), w_in.dtype), w_in[:, n_real:]],
                           axis=1).astype(BF16)


def _ret_log_decay():
    return np.log1p(-np.exp2(-5.0 - np.arange(RET_HEADS, dtype=np.float32))).astype(np.float32)


def _ret_head_out(o, g, gn):
    n = o * lax.rsqrt(jnp.mean(o * o, axis=-1, keepdims=True) + RMS_EPS) * gn
    return _silu(g) * n


def _ret_prompt_kernel(q_ref, k_ref, v_ref, g_ref, dm_ref, qd_ref, kd_ref, gn_ref, y_ref, sf_ref, s_scr, *, sdec):
    j = pl.program_id(1)

    @pl.when(j == 0)
    def _():
        s_scr[...] = jnp.zeros_like(s_scr)

    qb = q_ref[...]
    kb = k_ref[...]
    qd = (qb.astype(F32) * qd_ref[...]).astype(BF16)
    kd = (kb.astype(F32) * kd_ref[...]).astype(BF16)
    gn = gn_ref[...]
    for h in range(RET_HEADS):
        sl = slice(h * RET_DK, (h + 1) * RET_DK)
        vl = slice(h * RET_DV, (h + 1) * RET_DV)
        vh = v_ref[:, vl]
        s_h = s_scr[h]
        sc = _dot_nt(qb[:, sl], kb[:, sl]) * dm_ref[h]
        o = _dot(sc.astype(BF16), vh) + _dot(qd[:, sl], s_h.astype(BF16))
        s_scr[h] = sdec[h] * s_h + _dot_tn(kd[:, sl], vh)
        y_ref[:, vl] = _ret_head_out(o, g_ref[:, vl].astype(F32), gn).astype(y_ref.dtype)

    @pl.when(j == pl.num_programs(1) - 1)
    def _():
        sf_ref[...] = s_scr[...]


def _ret_tables(c):
    lg = _ret_log_decay()
    i = np.arange(c, dtype=np.float32)
    diff = i[:, None] - i[None, :]
    dm = np.where(diff[None] >= 0, np.exp(lg[:, None, None] * np.maximum(diff, 0.0)[None]), 0.0).astype(np.float32)
    qd = np.repeat(np.exp(lg[None, :] * (i[:, None] + 1.0)), RET_DK, axis=1).astype(np.float32)
    kd = np.repeat(np.exp(lg[None, :] * (c - 1.0 - i)[:, None]), RET_DK, axis=1).astype(np.float32)
    sdec = tuple(float(v) for v in np.exp(lg * np.float32(c)))
    return dm, qd, kd, sdec


def _ret_prompt(rq, rk, rv, rg, gn, *, batch, seq):
    c = 128
    n = seq // c
    dm, qd, kd, sdec = _ret_tables(c)
    row = lambda b, j: (b * n + j, 0)
    const2 = lambda b, j: (0, 0)
    return pl.pallas_call(
        functools.partial(_ret_prompt_kernel, sdec=sdec),
        grid=(batch, n),
        in_specs=[pl.BlockSpec((c, D_RET_QK), row), pl.BlockSpec((c, D_RET_QK), row),
                  pl.BlockSpec((c, D_RET_V), row), pl.BlockSpec((c, D_RET_V), row),
                  pl.BlockSpec((RET_HEADS, c, c), lambda b, j: (0, 0, 0)),
                  pl.BlockSpec((c, D_RET_QK), const2), pl.BlockSpec((c, D_RET_QK), const2),
                  pl.BlockSpec((1, RET_DV), const2)],
        out_specs=[pl.BlockSpec((c, D_RET_V), row),
                   pl.BlockSpec((None, RET_HEADS, RET_DK, RET_DV), lambda b, j: (b, 0, 0, 0))],
        out_shape=[jax.ShapeDtypeStruct((batch * seq, D_RET_V), BF16),
                   jax.ShapeDtypeStruct((batch, RET_HEADS, RET_DK, RET_DV), F32)],
        scratch_shapes=[pltpu.VMEM((RET_HEADS, RET_DK, RET_DV), F32)],
        compiler_params=_params("arbitrary", "arbitrary"),
        name="ret_prompt",
    )(rq, rk, rv, rg, jnp.asarray(dm), jnp.asarray(qd), jnp.asarray(kd), gn.reshape(1, RET_DV))


def _ret_sample_kernel(q_ref, k_ref, v_ref, g_ref, s0_ref, dm_ref, qd_ref, kd_ref, gn_ref, y_ref, sn_ref,
                       *, sdec, n_req, n_tok):
    rows = n_req * n_tok
    q = q_ref[...]
    k = k_ref[...]
    qd = q * qd_ref[...]
    kd = k * kd_ref[...]
    qb = q.astype(BF16)
    kb = k.astype(BF16)
    gn = gn_ref[...]
    req = lax.broadcasted_iota(I32, (rows, 1), 0) // n_tok
    for h in range(RET_HEADS):
        sl = slice(h * RET_DK, (h + 1) * RET_DK)
        vl = slice(h * RET_DV, (h + 1) * RET_DV)
        vh = v_ref[:, vl].astype(BF16)
        sc = _dot_nt(qb[:, sl], kb[:, sl]) * dm_ref[h]
        o = _dot(sc.astype(BF16), vh)
        qd_h = qd[:, sl]
        kd_h = kd[:, sl]
        for r in range(n_req):
            own = req == r
            s_r = s0_ref[r, h]
            o = o + _dot(jnp.where(own, qd_h, 0.0).astype(BF16), s_r.astype(BF16))
            sn_ref[r, h] = sdec[h] * s_r + _dot_tn(jnp.where(own, kd_h, 0.0).astype(BF16), vh)
        y_ref[:, vl] = _ret_head_out(o, g_ref[:, vl], gn).astype(y_ref.dtype)


def _ret_sample(rq, rk, rv, rg, state, gn, *, n_tok):
    db = state.shape[0]
    n_req = 16
    rows = n_req * n_tok
    lg = _ret_log_decay()
    i = np.arange(rows)
    t = (i % n_tok).astype(np.float32)
    same = (i[:, None] // n_tok) == (i[None, :] // n_tok)
    diff = t[:, None] - t[None, :]
    dm = np.where((same & (diff >= 0))[None], np.exp(lg[:, None, None] * np.maximum(diff, 0.0)[None]), 0.0)
    qd = np.repeat(np.exp(lg[None, :] * (t[:, None] + 1.0)), RET_DK, axis=1)
    kd = np.repeat(np.exp(lg[None, :] * (n_tok - 1.0 - t)[:, None]), RET_DK, axis=1)
    sdec = tuple(float(v) for v in np.exp(lg * np.float32(n_tok)))
    row = lambda i: (i, 0)
    const2 = lambda i: (0, 0)
    st_spec = pl.BlockSpec((n_req, RET_HEADS, RET_DK, RET_DV), lambda i: (i, 0, 0, 0))
    return pl.pallas_call(
        functools.partial(_ret_sample_kernel, sdec=sdec, n_req=n_req, n_tok=n_tok),
        grid=(db // n_req,),
        in_specs=[pl.BlockSpec((rows, D_RET_QK), row), pl.BlockSpec((rows, D_RET_QK), row),
                  pl.BlockSpec((rows, D_RET_V), row), pl.BlockSpec((rows, D_RET_V), row), st_spec,
                  pl.BlockSpec((RET_HEADS, rows, rows), lambda i: (0, 0, 0)),
                  pl.BlockSpec((rows, D_RET_QK), const2), pl.BlockSpec((rows, D_RET_QK), const2),
                  pl.BlockSpec((1, RET_DV), const2)],
        out_specs=[pl.BlockSpec((rows, D_RET_V), row), st_spec],
        out_shape=[jax.ShapeDtypeStruct((db * n_tok, D_RET_V), BF16),
                   jax.ShapeDtypeStruct(state.shape, F32)],
        compiler_params=_params("arbitrary"),
        name="ret_sample",
    )(rq, rk, rv, rg, state, jnp.asarray(dm, F32), jnp.asarray(qd, F32), jnp.asarray(kd, F32),
      gn.reshape(1, RET_DV))


def _cmp_weights(cmp_w1, cmp_pe, cmp_w2):
    w1c = jnp.concatenate([cmp_w1[:, :CMP_STRIDE], cmp_w1[:, CMP_STRIDE:]], axis=-1).astype(BF16)
    eye = jnp.eye(NSA_GROUPS, dtype=cmp_w2.dtype)
    w2bd = jnp.einsum('gk,chd->cghkd', eye, cmp_w2).reshape(2, NSA_GROUPS * CMP_HID, D_GK).astype(BF16)
    pe8 = jnp.broadcast_to(cmp_pe.reshape(2, 1, CMP_LEN * NSA_HEAD_DIM), (2, 8, CMP_LEN * NSA_HEAD_DIM))
    w1f = cmp_w1.reshape(2, CMP_LEN * NSA_HEAD_DIM, CMP_HID)
    r = np.arange(PAGE_SIZE)
    seg_pp = PAGE_SIZE // CMP_STRIDE
    perm = r[None, :] == (r[:, None] % seg_pp) * CMP_STRIDE + r[:, None] // seg_pp
    return jnp.asarray(perm, BF16), w1c, w2bd, jnp.swapaxes(w2bd, 1, 2), pe8, w1f


def _cmp_pe_term(pe8_ref, w1f_ref, c):
    return _dot3(pe8_ref[c], w1f_ref[c])[0:1]


def _cmp_partial(pages, perm, w1c_ref, c, feature_major):
    seg_pp = PAGE_SIZE // CMP_STRIDE
    if feature_major:
        ys = [_dot_nt(perm, pg[c * D_GK:(c + 1) * D_GK, :].astype(BF16)) for pg in pages]
    else:
        ys = [_dot(perm, pg[:, c * D_GK:(c + 1) * D_GK].astype(BF16)) for pg in pages]
    acc = None
    for p in range(CMP_STRIDE):
        xp = jnp.concatenate([y[p * seg_pp:(p + 1) * seg_pp] for y in ys], axis=0)
        xs = jnp.concatenate([xp[:, g * NSA_HEAD_DIM:(g + 1) * NSA_HEAD_DIM] for g in range(NSA_GROUPS)], axis=0)
        d = _dot(xs.astype(BF16), w1c_ref[c, p])
        acc = d if acc is None else acc + d
    return acc


def _cmp_store(pages, perm, w1c_ref, lohi_scr, step, feature_major=False):
    seg_ps = len(pages) * (PAGE_SIZE // CMP_STRIDE)
    off = pl.multiple_of(step * seg_ps, seg_ps)
    for c in range(2):
        acc = _cmp_partial(pages, perm, w1c_ref, c, feature_major)
        for g in range(NSA_GROUPS):
            lohi_scr[c, g, pl.ds(off, seg_ps), :] = acc[g * seg_ps:(g + 1) * seg_ps]


def _cmp_finish(lohi, pe, w2_c, feature_major):
    hs = []
    for g in range(NSA_GROUPS):
        lh = lohi(g)
        nseg = lh.shape[0]
        z = lh[:, :CMP_HID] + pltpu.roll(lh[:, CMP_HID:], nseg - 1, 0) + pe
        hs.append(_silu(z).astype(BF16))
    hid = jnp.concatenate(hs, axis=1)
    if feature_major:
        return _dot_nt(w2_c, hid)
    return _dot(hid, w2_c)


def _q_blockdiag(qf, tq):
    lane = lax.broadcasted_iota(I32, (tq, D_GK), 1)
    blocks = []
    for g in range(NSA_GROUPS):
        slab = qf[:, g * D_GK:(g + 1) * D_GK]
        keep = (lane >= g * NSA_HEAD_DIM) & (lane < (g + 1) * NSA_HEAD_DIM)
        for r in range(NSA_HPG):
            sh = (NSA_HEAD_DIM * (g - r)) % D_GK
            rolled = slab if sh == 0 else pltpu.roll(slab, sh, 1)
            blocks.append(jnp.where(keep, rolled, 0.0))
    return jnp.concatenate(blocks, axis=0).astype(BF16)


def _pair_pv(p, v):
    half = p.shape[0] // 2
    return jnp.concatenate([_dot(p[:half], v[:, :LANES]), _dot(p[half:], v[:, LANES:])], axis=0)


def _assemble_heads(acc, tq):
    lane = lax.broadcasted_iota(I32, (tq, LANES), 1)
    chunks = []
    for c in range(NSA_HEADS // 2):
        a = acc[(2 * c) * tq:(2 * c + 1) * tq]
        b = acc[(2 * c + 1) * tq:(2 * c + 2) * tq]
        if (c // 2) % 2 == 0:
            b = pltpu.roll(b, NSA_HEAD_DIM, 1)
        else:
            a = pltpu.roll(a, NSA_HEAD_DIM, 1)
        chunks.append(jnp.where(lane < NSA_HEAD_DIM, a, b))
    return jnp.concatenate(chunks, axis=1)


def _gate_branches(ng, eg_ref, branches, tq):
    nh, nl = _split(ng)
    out = None
    for br, acc in enumerate(branches):
        gexp = _dot(nh, eg_ref[br]) + _dot(nl, eg_ref[br])
        term = gexp * _assemble_heads(acc, tq)
        out = term if out is None else out + term
    return out


def _masked_softmax(s, mask):
    s = jnp.where(mask, s, NEG_INF)
    m = jnp.max(s, axis=-1, keepdims=True)
    e = jnp.exp(s - m)
    return e / jnp.sum(e, axis=-1, keepdims=True)


def _rank_select(sc, sc_scr, valid, n_rows):
    sc_scr[...] = sc
    n_io = lax.broadcasted_iota(I32, sc.shape, 0)

    def body(m, cnt):
        row = sc_scr[pl.ds(m, 1), :]
        beats = (row > sc) | ((row == sc) & (m < n_io))
        return cnt + jnp.where(beats, 1, 0)

    cnt = lax.fori_loop(0, n_rows, body, jnp.zeros(sc.shape, I32))
    return jnp.where((cnt < SEL_TOPK) & valid, 1.0, 0.0)


def _block_scores(scT, pos, n_io):
    valid = n_io * SEL_BLOCK <= pos
    cur = pos // SEL_BLOCK
    forced = (n_io == 0) | (n_io == cur) | (n_io == cur - 1)
    return jnp.where(forced, FORCE_SCORE, jnp.where(valid, scT, -FORCE_SCORE)), valid


def _nsa_tables(n_cmp_pad, n_cmp, n_blocks, nb_pad, n_keys_pad, tk, key_major=False):
    c = np.arange(n_cmp_pad)[None, :]
    n = np.arange(nb_pad)[:, None]
    ov = np.clip(np.minimum(CMP_STRIDE * c + CMP_LEN, SEL_BLOCK * n + SEL_BLOCK)
                 - np.maximum(CMP_STRIDE * c, SEL_BLOCK * n), 0, None) / CMP_STRIDE
    ov = np.where((c < n_cmp) & (n < n_blocks), ov, 0.0)
    key = np.arange(n_keys_pad)
    e = (key[None, :] // SEL_BLOCK == np.arange(LANES * ((nb_pad + LANES - 1) // LANES))[:, None])
    e3 = e.reshape(e.shape[0], n_keys_pad // tk, tk).transpose(1, 0, 2)
    if key_major:
        e3 = e3.transpose(0, 2, 1)
    col = np.arange(LANES)[:, None]
    lane = np.arange(D_NSA)[None, :]
    eg = np.stack([col == br * NSA_HEADS + lane // NSA_HEAD_DIM for br in range(3)])
    return jnp.asarray(ov, BF16), jnp.asarray(e3, BF16), jnp.asarray(eg, BF16)


def _cmp_prompt_kernel(ck_ref, perm_ref, w1c_ref, w2_ref, w2t_ref, pe8_ref, w1f_ref, ko_ref, vto_ref, lohi_scr):
    st = pl.program_id(1)
    n_pg = ck_ref.shape[1] // PAGE_SIZE
    pages = [ck_ref[:, k * PAGE_SIZE:(k + 1) * PAGE_SIZE] for k in range(n_pg)]
    _cmp_store(pages, perm_ref[...], w1c_ref, lohi_scr, st, feature_major=True)

    @pl.when(st == pl.num_programs(1) - 1)
    def _():
        ko_ref[...] = _cmp_finish(lambda g: lohi_scr[0, g], _cmp_pe_term(pe8_ref, w1f_ref, 0), w2_ref[0],
                                  False).astype(ko_ref.dtype)
        vto_ref[...] = _cmp_finish(lambda g: lohi_scr[1, g], _cmp_pe_term(pe8_ref, w1f_ref, 1), w2t_ref[1],
                                   True).astype(vto_ref.dtype)


def _cmp_prompt(ckv_t, cmp_w, *, batch, seq):
    nseg = seq // CMP_STRIDE
    rows = 8 * PAGE_SIZE
    n_st = seq // rows
    full = lambda a: pl.BlockSpec(a.shape, lambda b, s: (0,) * a.ndim)
    return pl.pallas_call(
        _cmp_prompt_kernel,
        grid=(batch, n_st),
        in_specs=[pl.BlockSpec((None, D_KV, rows), lambda b, s: (b, 0, s))] + [full(a) for a in cmp_w],
        out_specs=[pl.BlockSpec((nseg, D_GK), lambda b, s: (b, 0)),
                   pl.BlockSpec((None, D_GK, nseg), lambda b, s: (b, 0, 0))],
        out_shape=[jax.ShapeDtypeStruct((batch * nseg, D_GK), BF16),
                   jax.ShapeDtypeStruct((batch, D_GK, nseg), BF16)],
        scratch_shapes=[pltpu.VMEM((2, NSA_GROUPS, nseg, 2 * CMP_HID), F32)],
        compiler_params=_params("arbitrary", "arbitrary"),
        name="cmp_prompt",
    )(ckv_t, *cmp_w)


def _col_softmax(s, mask):
    s = jnp.where(mask, s, NEG_INF)
    return jnp.exp2(s - jnp.max(s, axis=0, keepdims=True))


def _nsa_prompt_kernel(qt_ref, ng_ref, ck_ref, cvt_ref, sk_ref, svt_ref, wk_ref, wvt_ref, ov_ref, et_ref,
                       o_ref, sc_scr, *gscr, tq, tk, wlen):
    seq = sk_ref.shape[0]
    nseg = ck_ref.shape[0]
    nb = ov_ref.shape[0]
    gl = NSA_HPG * tq
    hd = NSA_HEAD_DIM
    q0 = pl.program_id(1) * tq

    qt = qt_ref[0]
    qg = []
    for g in range(NSA_GROUPS):
        cols = []
        for r in range(NSA_HPG):
            h = NSA_HPG * g + r
            parts = [qt[h * hd:(h + 1) * hd, :]]
            if g > 0:
                parts.insert(0, jnp.zeros((hd * g, tq), BF16))
            if g < NSA_GROUPS - 1:
                parts.append(jnp.zeros((hd * (NSA_GROUPS - 1 - g), tq), BF16))
            cols.append(jnp.concatenate(parts, axis=0))
        qg.append(jnp.concatenate(cols, axis=1))
    pos_t = q0 + lax.broadcasted_iota(I32, (1, gl), 1) % tq

    ck = ck_ref[...]
    cidx = lax.broadcasted_iota(I32, (nseg, gl), 0)
    cmask = (CMP_STRIDE * cidx + (CMP_LEN - 1) <= pos_t) & (cidx < nseg - 1)
    o_cmp, pgs = [], []
    ss = [_dot(ck, qg[g]) for g in range(NSA_GROUPS)]
    for g in range(NSA_GROUPS):
        e = _col_softmax(ss[g], cmask)
        p = jnp.where(cmask, e * (1.0 / jnp.sum(e, axis=0, keepdims=True)), 0.0)
        pgs.append(p[:, 0:tq] + p[:, tq:2 * tq] + p[:, 2 * tq:3 * tq] + p[:, 3 * tq:4 * tq])
        ss[g] = p.astype(BF16)
    for g in range(NSA_GROUPS):
        o_cmp.append(_dot(cvt_ref[g * hd:(g + 1) * hd, :], ss[g]))
    ph, plo = _split(jnp.concatenate(pgs, axis=1))
    sc_t = _dot(ov_ref[...], ph) + _dot(ov_ref[...], plo)

    n_io = lax.broadcasted_iota(I32, sc_t.shape, 0)
    pos_l = q0 + lax.broadcasted_iota(I32, sc_t.shape, 1) % tq
    sc, valid = _block_scores(sc_t, pos_l, n_io)
    sel_t = _rank_select(sc, sc_scr, valid, jnp.minimum((q0 + tq - 1) // SEL_BLOCK + 1, nb))
    selb = jnp.concatenate([sel_t, jnp.zeros((LANES - nb, sc_t.shape[1]), F32)], axis=0).astype(BF16)

    m_scr, acc_scr = gscr[:NSA_GROUPS], gscr[NSA_GROUPS:]
    for g in range(NSA_GROUPS):
        m_scr[g][...] = jnp.full(m_scr[g].shape, NEG_INF, F32)
        acc_scr[g][...] = jnp.zeros(acc_scr[g].shape, F32)
    key_i = lax.broadcasted_iota(I32, (tk, tq), 0)
    pos_q = q0 + lax.broadcasted_iota(I32, (tk, tq), 1)
    ones_k = jnp.ones((ONES_ROWS, tk), BF16)

    def kv_step(j, carry):
        k0 = pl.multiple_of(j * tk, tk)
        kt = sk_ref[pl.ds(k0, tk), :]
        selm = _dot(et_ref[j], selb)
        causal = k0 + key_i <= pos_q
        ss, ms = [], []
        for g in range(NSA_GROUPS):
            bias = jnp.where((selm[:, g * tq:(g + 1) * tq] > 0.5) & causal, 0.0, NEG_INF)
            s = _dot(kt, qg[g]) + jnp.concatenate([bias] * NSA_HPG, axis=1)
            ss.append(s)
            ms.append(jnp.maximum(m_scr[g][...], jnp.max(s, axis=0, keepdims=True)))
        ps = [jnp.exp2(ss[g] - ms[g]).astype(BF16) for g in range(NSA_GROUPS)]
        for g in range(NSA_GROUPS):
            vt = jnp.concatenate([svt_ref[j * (tk // LANES) + u, g * hd:(g + 1) * hd, :]
                                  for u in range(tk // LANES)], axis=1)
            alpha = jnp.exp2(m_scr[g][...] - ms[g])
            acc_scr[g][...] = alpha * acc_scr[g][...] + _dot(jnp.concatenate([vt, ones_k], axis=0), ps[g])
            m_scr[g][...] = ms[g]
        return carry

    lax.fori_loop(0, (q0 + tq + tk - 1) // tk, kv_step, 0)

    w0 = pl.multiple_of(jnp.clip(q0 + tq - wlen, 0, seq - wlen), LANES)
    kw = wk_ref[pl.ds(w0, wlen), :]
    dist = pos_t - (w0 + lax.broadcasted_iota(I32, (wlen, gl), 0))
    wmask = (dist >= 0) & (dist < WINDOW)
    o_win = []
    ones_w = jnp.ones((ONES_ROWS, wlen), BF16)
    ss = [_dot(kw, qg[g]) for g in range(NSA_GROUPS)]
    ss = [_col_softmax(ss[g], wmask).astype(BF16) for g in range(NSA_GROUPS)]
    for g in range(NSA_GROUPS):
        vt = jnp.concatenate([wvt_ref[w0 // LANES + u, g * hd:(g + 1) * hd, :] for u in range(wlen // LANES)], axis=1)
        ow = _dot(jnp.concatenate([vt, ones_w], axis=0), ss[g])
        o_win.append(ow[:hd] * (1.0 / ow[hd:hd + 1]))

    ngt = ng_ref[...].T
    heads = []
    for g in range(NSA_GROUPS):
        acc = acc_scr[g][...]
        o_sel = acc[:hd] * (1.0 / acc[hd:hd + 1])
        for r in range(NSA_HPG):
            h = NSA_HPG * g + r
            lanes = slice(r * tq, (r + 1) * tq)
            heads.append(ngt[h:h + 1, :] * o_cmp[g][:, lanes]
                         + ngt[NSA_HEADS + h:NSA_HEADS + h + 1, :] * o_sel[:, lanes]
                         + ngt[2 * NSA_HEADS + h:2 * NSA_HEADS + h + 1, :] * o_win[g][:, lanes])
    ot = jnp.concatenate(heads, axis=0)
    o_ref[...] = jnp.concatenate([ot[c * LANES:(c + 1) * LANES].T for c in range(D_NSA // LANES)],
                                 axis=1).astype(o_ref.dtype)


def _nsa_prompt(nqt, ng, ck, cvt, skb, svt, wkb, wvt, *, batch, seq):
    tq, tk, wlen = LANES, 256, 640
    assert seq % tk == 0 and seq >= wlen and wlen >= WINDOW + tq
    nseg = seq // CMP_STRIDE
    n_blocks = seq // SEL_BLOCK
    nq_t = seq // tq
    ov, et, _ = _nsa_tables(nseg, nseg - 1, n_blocks, n_blocks, seq, tk, key_major=True)
    gl = NSA_HPG * tq
    qrow = lambda b, i: (b * nq_t + i, 0)
    full = lambda a: pl.BlockSpec(a.shape, lambda b, i: (0,) * a.ndim)
    kspec = pl.BlockSpec((seq, D_GK), lambda b, i: (b, 0))
    vtspec = pl.BlockSpec((seq // LANES, D_GK, LANES), lambda b, i: (b, 0, 0))
    return pl.pallas_call(
        functools.partial(_nsa_prompt_kernel, tq=tq, tk=tk, wlen=wlen),
        grid=(batch, nq_t),
        in_specs=[pl.BlockSpec((1, D_NSA, tq), lambda b, i: (b * nq_t + i, 0, 0)), pl.BlockSpec((tq, LANES), qrow),
                  pl.BlockSpec((nseg, D_GK), lambda b, i: (b, 0)),
                  pl.BlockSpec((None, D_GK, nseg), lambda b, i: (b, 0, 0)),
                  kspec, vtspec, kspec, vtspec, full(ov), full(et)],
        out_specs=pl.BlockSpec((tq, D_NSA), qrow),
        out_shape=jax.ShapeDtypeStruct((batch * seq, D_NSA), BF16),
        scratch_shapes=[pltpu.VMEM((n_blocks, NSA_GROUPS * tq), F32)]
        + [pltpu.VMEM((1, gl), F32)] * NSA_GROUPS
        + [pltpu.VMEM((NSA_HEAD_DIM + ONES_ROWS, gl), F32)] * NSA_GROUPS,
        compiler_params=_params("arbitrary", "arbitrary"),
        name="nsa_prompt",
    )(nqt, ng, ck, cvt, skb, svt, wkb, wvt, ov, et)


def _nsa_sample_kernel(pt_ref, *refs, n_tok, ppb, n_blocks, past):
    cmp_refs = refs[:ppb]
    sel_refs = refs[ppb:2 * ppb]
    (q_ref, ng_ref, snew_ref, wnew_ref, wint_ref, perm_ref, w1c_ref, w2_ref, w2t_ref, pe8_ref, w1f_ref, ov_ref,
     e_ref, eg_ref, o_ref, wout_ref, lohi_scr, sel_scr, s_scr, pe_scr, sc_scr) = refs[2 * ppb:]
    del pt_ref
    b = pl.program_id(0)
    st = pl.program_id(1)
    n_st = sel_scr.shape[0]
    nseg = lohi_scr.shape[2]
    tq = n_tok
    rows = NSA_HEADS * tq
    tkp = ppb * PAGE_SIZE

    @pl.when((b == 0) & (st == 0))
    def _():
        for c in range(2):
            pe_scr[c] = jnp.broadcast_to(_cmp_pe_term(pe8_ref, w1f_ref, c), (8, CMP_HID))

    _cmp_store([r[...] for r in cmp_refs], perm_ref[...], w1c_ref, lohi_scr, st, feature_major=True)
    for k, r in enumerate(sel_refs):
        sel_scr[st, :, k * PAGE_SIZE:(k + 1) * PAGE_SIZE] = r[...].astype(BF16)

    @pl.when(st == n_st - 1)
    def _():
        qbd = _q_blockdiag(q_ref[...], tq)
        pos_r = past + lax.broadcasted_iota(I32, (rows, 1), 0) % tq
        pad = jnp.zeros((LANES - tq, D_KV), F32)
        snew_t = jnp.concatenate([snew_ref[...], pad], axis=0).T.astype(BF16)
        wnew_t = jnp.concatenate([wnew_ref[...], pad], axis=0).T

        kct = _cmp_finish(lambda g: lohi_scr[0, g], pe_scr[0, 0:1], w2t_ref[0], True).astype(BF16)
        vc = _cmp_finish(lambda g: lohi_scr[1, g], pe_scr[1, 0:1], w2_ref[1], False).astype(BF16)
        s = _dot(qbd, kct)
        cidx = lax.broadcasted_iota(I32, (rows, nseg), 1)
        cmask = (CMP_STRIDE * cidx + (CMP_LEN - 1) <= pos_r) & (cidx < nseg - 1)
        p = jnp.where(cmask, _masked_softmax(s, cmask), 0.0)
        o_cmp = _pair_pv(p.astype(BF16), vc)
        pg = jnp.sum(p.reshape(NSA_GROUPS, NSA_HPG, tq, nseg), axis=1, keepdims=True)
        pg = jnp.broadcast_to(pg, (NSA_GROUPS, NSA_HPG, tq, nseg)).reshape(rows, nseg)
        ph, plo = _split(pg)
        sc_t = _dot_nt(ov_ref[...], ph) + _dot_nt(ov_ref[...], plo)

        nb_pad = ov_ref.shape[0]
        nb8 = sc_scr.shape[0]
        n_io = lax.broadcasted_iota(I32, (nb8, rows), 0)
        pos_l = past + lax.broadcasted_iota(I32, (nb8, rows), 1) % tq
        sc, valid = _block_scores(sc_t[:nb8], pos_l, n_io)
        sel_t = _rank_select(sc, sc_scr, valid & (n_io < n_blocks), n_blocks)
        sel = jnp.concatenate([sel_t, jnp.zeros((nb_pad - nb8, rows), F32)], axis=0).T.astype(BF16)

        for t in range(n_st):
            selm = _dot(sel, e_ref[t])
            s_scr[:, t * tkp:(t + 1) * tkp] = jnp.where(selm > 0.5, _dot(qbd, sel_scr[t, 0:D_GK, :]), NEG_INF)
        selm = _dot(sel, e_ref[n_st, :, 0:LANES])
        new_ok = (selm > 0.5) & (past + lax.broadcasted_iota(I32, (rows, LANES), 1) <= pos_r)
        s_scr[:, n_st * tkp:] = jnp.where(new_ok, _dot(qbd, snew_t[0:D_GK]), NEG_INF)
        s_all = s_scr[...]
        ps = jnp.exp(s_all - jnp.max(s_all, axis=-1, keepdims=True))
        l_sel = jnp.sum(ps, axis=-1, keepdims=True)
        ps = ps.astype(BF16)
        acc = _dot_nt(ps[:, n_st * tkp:], snew_t[D_GK:])
        for t in range(n_st):
            acc = acc + _dot_nt(ps[:, t * tkp:(t + 1) * tkp], sel_scr[t, D_GK:, :])
        acc = acc * (1.0 / l_sel)
        half = rows // 2
        o_sel = jnp.concatenate([acc[:half, :LANES], acc[half:, LANES:]], axis=0)

        win_buf = wint_ref.shape[1]
        wt = wint_ref[...]
        wtb = wt.astype(BF16)
        wnb = wnew_t.astype(BF16)
        sw = jnp.concatenate([_dot(qbd, wtb[0:D_GK]), _dot(qbd, wnb[0:D_GK])], axis=1)
        k_pos = past - win_buf + lax.broadcasted_iota(I32, sw.shape, 1)
        dist = pos_r - k_pos
        pw = _masked_softmax(sw, (dist >= 0) & (dist < WINDOW) & (k_pos >= 0)).astype(BF16)
        accw = _dot_nt(pw[:, :win_buf], wtb[D_GK:]) + _dot_nt(pw[:, win_buf:], wnb[D_GK:])
        o_win = jnp.concatenate([accw[:half, :LANES], accw[half:, LANES:]], axis=0)

        o_ref[...] = _gate_branches(ng_ref[...], eg_ref, (o_cmp, o_sel, o_win), tq)

        shifted = pltpu.roll(wt, win_buf - tq, 1)
        newr = pltpu.roll(wnew_t, LANES - tq, 1)
        lane = lax.broadcasted_iota(I32, (D_KV, LANES), 1)
        wout_ref[:, 0:win_buf - LANES] = shifted[:, 0:win_buf - LANES]
        wout_ref[:, win_buf - LANES:] = jnp.where(lane >= LANES - tq, newr, shifted[:, win_buf - LANES:])


def _nsa_sample(nq, ng, skv, wkv, cache_cmp_t, cache_sel_t, cache_win_t, page_table, cmp_w, *, n_tok):
    db, n_pages = page_table.shape
    past = n_pages * PAGE_SIZE
    win_buf = cache_win_t.shape[2]
    ppb = 8
    tkp = ppb * PAGE_SIZE
    assert n_tok == 8 and n_tok < CMP_STRIDE and n_pages % ppb == 0 and win_buf % LANES == 0 and win_buf >= LANES
    n_st = n_pages // ppb
    nseg = past // CMP_STRIDE
    n_blocks = -(-(past + n_tok) // SEL_BLOCK)
    nb_pad = -(-n_blocks // LANES) * LANES
    ov, e3, eg = _nsa_tables(nseg, nseg - 1, n_blocks, nb_pad, (n_st + 1) * tkp, tkp)
    rows = NSA_HEADS * n_tok

    def page_spec(k):
        return pl.BlockSpec((None, D_KV, PAGE_SIZE), lambda b, s, pt: (pt[b, s * ppb + k], 0, 0))

    req = lambda cols: pl.BlockSpec((n_tok, cols), lambda b, s, pt: (b, 0))
    full = lambda a: pl.BlockSpec(a.shape, lambda b, s, pt: (0,) * a.ndim)
    win_spec = pl.BlockSpec((None, D_KV, win_buf), lambda b, s, pt: (b, 0, 0))
    grid_spec = pltpu.PrefetchScalarGridSpec(
        num_scalar_prefetch=1,
        grid=(db, n_st),
        in_specs=[page_spec(k) for k in range(ppb)] + [page_spec(k) for k in range(ppb)]
        + [req(D_NSA), req(LANES), req(D_KV), req(D_KV), win_spec]
        + [full(a) for a in cmp_w]
        + [full(ov), pl.BlockSpec(e3.shape, lambda b, s, pt: (0, 0, 0), pipeline_mode=pl.Buffered(1)), full(eg)],
        out_specs=[req(D_NSA), win_spec],
        scratch_shapes=[pltpu.VMEM((2, NSA_GROUPS, nseg, 2 * CMP_HID), F32),
                        pltpu.VMEM((n_st, D_KV, tkp), BF16),
                        pltpu.VMEM((rows, n_st * tkp + LANES), F32),
                        pltpu.VMEM((2, 8, CMP_HID), F32),
                        pltpu.VMEM((-(-n_blocks // 8) * 8, rows), F32)],
    )
    return pl.pallas_call(
        functools.partial(_nsa_sample_kernel, n_tok=n_tok, ppb=ppb, n_blocks=n_blocks, past=past),
        grid_spec=grid_spec,
        out_shape=[jax.ShapeDtypeStruct((db * n_tok, D_NSA), F32),
                   jax.ShapeDtypeStruct((db, D_KV, win_buf), F32)],
        compiler_params=_params("arbitrary", "arbitrary"),
        name="nsa_sample",
    )(page_table, *([cache_cmp_t] * ppb), *([cache_sel_t] * ppb), nq, ng, skv, wkv, cache_win_t,
      *cmp_w, ov, e3, eg)


def _route(lg):
    lane = lax.broadcasted_iota(I32, lg.shape, 1).astype(F32)
    big = float(1 << 20)
    is_g = lane < N_GROUPS
    gl = jnp.where(is_g, lg, -jnp.inf)
    gmax = jnp.max(gl, axis=-1, keepdims=True)
    g_star = jnp.min(jnp.where(gl == gmax, lane, big), axis=-1, keepdims=True)
    g_w = 1.0 / jnp.sum(jnp.where(is_g, jnp.exp(lg - gmax), 0.0), axis=-1, keepdims=True)
    lo = N_GROUPS + g_star * EXPERTS_PER_GROUP
    el = jnp.where((lane >= lo) & (lane < lo + EXPERTS_PER_GROUP), lg, -jnp.inf)
    m1 = jnp.max(el, axis=-1, keepdims=True)
    i1 = jnp.min(jnp.where(el == m1, lane, big), axis=-1, keepdims=True)
    el2 = jnp.where(lane == i1, -jnp.inf, el)
    m2 = jnp.max(el2, axis=-1, keepdims=True)
    i2 = jnp.min(jnp.where(el2 == m2, lane, big), axis=-1, keepdims=True)
    e2 = jnp.exp(m2 - m1)
    p1 = 1.0 / (1.0 + e2)
    out = jnp.where(lane == 0, i1 - N_GROUPS, 0.0)
    out = jnp.where(lane == 1, i2 - N_GROUPS, out)
    out = jnp.where(lane == 2, g_w * p1, out)
    return jnp.where(lane == 3, g_w * (e2 * p1), out)


def _mixout_kernel(x_ref, yr_ref, on_ref, mg_ref, gt_ref, sh_ref, sc_ref, g2_ref, wr_ref, wn_ref, wo_ref,
                   wrt_ref, brt_ref, x1_ref, h2_ref, rt_ref):
    y_ret = _dot(yr_ref[...], wr_ref[...])
    y_nsa = _dot(on_ref[...].astype(BF16), wn_ref[...])
    mg = mg_ref[...].astype(F32)
    z = mg[:, :D_MODEL] * y_ret + mg[:, D_MODEL:] * y_nsa
    x1 = x_ref[...] + gt_ref[...] * _dot(z.astype(BF16), wo_ref[...])
    x1_ref[...] = x1
    h = x1 * lax.rsqrt(jnp.mean(x1 * x1, axis=-1, keepdims=True) + RMS_EPS) * g2_ref[...]
    h2 = h * (1.0 + sc_ref[...]) + sh_ref[...]
    h2_ref[...] = h2
    rt_ref[...] = _route(_dot3(h2, wrt_ref[...]) + brt_ref[...])


def _mixout(x, yret, onsa, mg, gate, shift, scale, g2, w_ret_o, w_nsa_o, w_out, w_rt, b_rt, *, tm, rows_per_mod):
    t, d = x.shape
    r = gate.shape[1]
    row = lambda cols: pl.BlockSpec((tm, cols), lambda i: (i, 0))
    mod_spec = pl.BlockSpec((None, r, d), lambda i: (i // rows_per_mod, 0, 0))
    full = lambda a: pl.BlockSpec(a.shape, lambda i: (0,) * a.ndim)
    return pl.pallas_call(
        _mixout_kernel,
        grid=(t // tm,),
        in_specs=[row(d), row(D_RET_V), row(D_NSA), row(2 * d), mod_spec, mod_spec, mod_spec,
                  pl.BlockSpec((1, d), lambda i: (0, 0)), full(w_ret_o), full(w_nsa_o), full(w_out),
                  full(w_rt), full(b_rt)],
        out_specs=[row(d), row(d), row(LANES)],
        out_shape=[jax.ShapeDtypeStruct((t, d), F32), jax.ShapeDtypeStruct((t, d), F32),
                   jax.ShapeDtypeStruct((t, LANES), F32)],
        compiler_params=_params("arbitrary"),
        name="mixout",
    )(x, yret, onsa, mg, gate, shift, scale, g2.reshape(1, d), w_ret_o, w_nsa_o, w_out, w_rt, b_rt)


def _moe_kernel(be_ref, rt_ref, nu_ref, x_hbm, w1_ref, w3_ref, w2_ref, y_ref, xbuf, sem, *, bm):
    i = pl.program_id(0)
    n_used = nu_ref[0]

    def start_rows(blk, slot):
        for r in range(bm):
            tok = rt_ref[blk * bm + r]
            pltpu.make_async_copy(x_hbm.at[pl.ds(tok, 1), :], xbuf.at[slot, pl.ds(r, 1), :], sem.at[slot]).start()

    def wait_rows(slot):
        pltpu.make_async_copy(x_hbm.at[pl.ds(0, bm), :], xbuf.at[slot], sem.at[slot]).wait()

    @pl.when((i == 0) & (n_used > 0))
    def _():
        start_rows(0, 0)

    @pl.when(i < n_used)
    def _():
        slot = i % 2
        wait_rows(slot)
        xb = xbuf[slot].astype(BF16)
        start_rows(jnp.minimum(i + 1, n_used - 1), 1 - slot)
        hid = _silu(_dot(xb, w1_ref[...])) * _dot(xb, w3_ref[...])
        y_ref[...] = _dot(hid.astype(BF16), w2_ref[...])

    @pl.when(i == n_used - 1)
    def _():
        wait_rows(1 - i % 2)

    @pl.when(i >= n_used)
    def _():
        y_ref[...] = jnp.zeros(y_ref.shape, F32)


def _moe_experts(h2, blk_expert, row_tok, n_used, w_e1, w_e3, w_e2, *, bm):
    t, d = h2.shape
    n_blk = blk_expert.shape[0]
    de = w_e1.shape[-1]
    grid_spec = pltpu.PrefetchScalarGridSpec(
        num_scalar_prefetch=3,
        grid=(n_blk,),
        in_specs=[pl.BlockSpec(memory_space=pl.ANY),
                  pl.BlockSpec((None, d, de), lambda i, be, rt, nu: (be[i], 0, 0)),
                  pl.BlockSpec((None, d, de), lambda i, be, rt, nu: (be[i], 0, 0)),
                  pl.BlockSpec((None, de, d), lambda i, be, rt, nu: (be[i], 0, 0))],
        out_specs=pl.BlockSpec((bm, d), lambda i, be, rt, nu: (i, 0)),
        scratch_shapes=[pltpu.VMEM((2, bm, d), F32), pltpu.SemaphoreType.DMA((2,))],
    )
    return pl.pallas_call(
        functools.partial(_moe_kernel, bm=bm),
        grid_spec=grid_spec,
        out_shape=jax.ShapeDtypeStruct((n_blk * bm, d), F32),
        compiler_params=_params("arbitrary"),
        name="moe_experts",
    )(blk_expert, row_tok, n_used, h2, w_e1, w_e3, w_e2)


def _dispatch_plan(expert, *, bm):
    t = expert.shape[0]
    tk = t * EXPERT_TOPK
    e_flat = expert.reshape(-1)
    onehot = (e_flat[:, None] == jnp.arange(N_EXPERTS, dtype=I32)[None, :]).astype(I32)
    csum = jnp.cumsum(onehot, axis=0)
    counts = csum[-1]
    rank = jnp.sum((csum - onehot) * onehot, axis=1)
    padded = (counts + bm - 1) // bm * bm
    pad_end = jnp.cumsum(padded)
    pad_start = pad_end - padded
    dest = pad_start[e_flat] + rank
    n_blk = -(-tk // bm) + N_EXPERTS
    blk_start = jnp.arange(n_blk, dtype=I32) * bm
    blk_expert = jnp.minimum(jnp.sum((pad_end[None, :] <= blk_start[:, None]).astype(I32), axis=1), N_EXPERTS - 1)
    row_tok = jnp.zeros((n_blk * bm,), I32).at[dest].set(jnp.arange(tk, dtype=I32) // EXPERT_TOPK)
    n_used = (pad_end[-1:] // bm).astype(I32)
    return dest.reshape(t, EXPERT_TOPK), blk_expert, row_tok, n_used


def _final_kernel(ds_ref, x1_ref, rt_ref, gt_ref, gf_ref, y_hbm, o_ref, ybuf, sem, *, tm, row0):
    i = pl.program_id(0)
    n = pl.num_programs(0)

    def start_rows(blk, slot):
        for r in range(tm):
            for k in range(EXPERT_TOPK):
                src = ds_ref[(row0 + blk * tm + r) * EXPERT_TOPK + k]
                pltpu.make_async_copy(y_hbm.at[pl.ds(src, 1), :], ybuf.at[slot, k, pl.ds(r, 1), :],
                                      sem.at[slot]).start()

    def wait_rows(slot):
        for k in range(EXPERT_TOPK):
            pltpu.make_async_copy(y_hbm.at[pl.ds(0, tm), :], ybuf.at[slot, k], sem.at[slot]).wait()

    @pl.when(i == 0)
    def _():
        start_rows(0, 0)

    slot = i % 2
    wait_rows(slot)
    y0 = ybuf[slot, 0]
    y1 = ybuf[slot, 1]
    start_rows(jnp.minimum(i + 1, n - 1), 1 - slot)
    rt = rt_ref[...]
    x2 = x1_ref[...] + gt_ref[...] * (rt[:, 2:3] * y0 + rt[:, 3:4] * y1)
    o_ref[...] = x2 * lax.rsqrt(jnp.mean(x2 * x2, axis=-1, keepdims=True) + RMS_EPS) * gf_ref[...]

    @pl.when(i == n - 1)
    def _():
        wait_rows(1 - slot)


def _final(dest_flat, x1, route, gate, g_final, y_buf, *, tm, rows_per_mod, row0):
    t, d = x1.shape
    r = gate.shape[1]
    grid_spec = pltpu.PrefetchScalarGridSpec(
        num_scalar_prefetch=1,
        grid=(t // tm,),
        in_specs=[pl.BlockSpec((tm, d), lambda i, ds: (i, 0)), pl.BlockSpec((tm, LANES), lambda i, ds: (i, 0)),
                  pl.BlockSpec((None, r, d), lambda i, ds: (i // rows_per_mod, 0, 0)),
                  pl.BlockSpec((1, d), lambda i, ds: (0, 0)), pl.BlockSpec(memory_space=pl.ANY)],
        out_specs=pl.BlockSpec((tm, d), lambda i, ds: (i, 0)),
        scratch_shapes=[pltpu.VMEM((2, EXPERT_TOPK, tm, d), F32), pltpu.SemaphoreType.DMA((2,))],
    )
    return pl.pallas_call(
        functools.partial(_final_kernel, tm=tm, row0=row0),
        grid_spec=grid_spec,
        out_shape=jax.ShapeDtypeStruct((t, d), F32),
        compiler_params=_params("arbitrary"),
        name="final",
    )(dest_flat, x1, route, gate, g_final.reshape(1, d), y_buf)


def kernel(x_prompt, x_sample, c_prompt, c_sample, state_ret, cache_cmp_kv, cache_sel_kv, cache_win_kv, page_table,
           w_ada, b_ada, g_norm1, g_norm2, g_final, w_in, ret_norm_g, cmp_w1, cmp_pe, cmp_w2, w_ret_o, w_nsa_o,
           w_out, w_grp, b_grp, w_exp, b_exp, w_e1, w_e3, w_e2):
    depth = w_in.shape[0]
    assert depth == 1, "one layer"
    batch, seq, d = x_prompt.shape
    db, n_tok, _ = x_sample.shape
    n_pages = page_table.shape[1]
    past = n_pages * PAGE_SIZE
    win_buf = cache_win_kv.shape[2]
    assert win_buf <= seq
    tp = batch * seq
    ts = db * n_tok
    l = 0

    mod = _adaln(jnp.concatenate([c_prompt, c_sample], axis=0).astype(F32), w_ada[l], b_ada[l])
    mods = jnp.split(mod, 6, axis=-1)
    mod_p = [m[:batch].reshape(batch, 1, d) for m in mods]
    ts_tile = min(ts, 256)
    mod_s = [jnp.repeat(m[batch:], n_tok, axis=0).reshape(ts // ts_tile, ts_tile, d) for m in mods]

    w_pad = _pad_w_in(w_in[l])
    w_t = w_in[l][:, C_NQ:C_NG].T.astype(BF16)
    cmp_w = _cmp_weights(cmp_w1[l], cmp_pe[l], cmp_w2[l])
    wr, wn, wo = w_ret_o[l].astype(BF16), w_nsa_o[l].astype(BF16), w_out[l].astype(BF16)
    w_rt = jnp.concatenate([w_grp[l], w_exp[l], jnp.zeros((d, LANES - N_GROUPS - N_EXPERTS), F32)], axis=1)
    b_rt = jnp.concatenate([b_grp[l], b_exp[l], jnp.zeros((LANES - N_GROUPS - N_EXPERTS,), F32)]).reshape(1, LANES)

    tm_p = 256
    cos_p, sin_p = _rope_tables(jnp.arange(seq))
    (rq, rk, rv, rg, nqt, ckv_pt, skv_pt, wkv_pt, skb, wkb, ng, mg, svt, wvt) = _inproj(
        x_prompt.reshape(tp, d).astype(F32), mod_p[0], mod_p[1], g_norm1[l], cos_p, sin_p, w_pad, w_t,
        tm=tm_p, rows_per_mod=seq // tm_p, pos_blocks=seq // tm_p, act_dtype=BF16)
    yret_p, ret_state_p = _ret_prompt(rq, rk, rv, rg, ret_norm_g[l], batch=batch, seq=seq)
    ck, cvt = _cmp_prompt(ckv_pt, cmp_w, batch=batch, seq=seq)
    onsa_p = _nsa_prompt(nqt, ng, ck, cvt, skb, svt, wkb, wvt, batch=batch, seq=seq)
    tm_m = 512
    x1_p, h2_p, route_p = _mixout(x_prompt.reshape(tp, d).astype(F32), yret_p, onsa_p, mg, mod_p[2], mod_p[3],
                                  mod_p[4], g_norm2[l], wr, wn, wo, w_rt, b_rt, tm=tm_m, rows_per_mod=seq // tm_m)

    cos_s, sin_s = _rope_tables(past + jnp.arange(n_tok))
    cos_s = jnp.tile(cos_s, (ts_tile // n_tok, 1))
    sin_s = jnp.tile(sin_s, (ts_tile // n_tok, 1))
    (rq_s, rk_s, rv_s, rg_s, nq_s, ckv_s, skv_s, wkv_s, ng_s, mg_s) = _inproj(
        x_sample.reshape(ts, d).astype(F32), mod_s[0], mod_s[1], g_norm1[l], cos_s, sin_s, w_pad, None,
        tm=ts_tile, rows_per_mod=1, pos_blocks=1, act_dtype=F32)
    yret_s, ret_state_s = _ret_sample(rq_s, rk_s, rv_s, rg_s, state_ret[l].astype(F32), ret_norm_g[l], n_tok=n_tok)

    def feature_major(cache):
        n, r = cache.shape[:2]
        return jnp.transpose(cache, (0, 2, 3, 4, 1)).reshape(n, D_KV, r).astype(F32)

    onsa_s, win_s_t = _nsa_sample(nq_s, ng_s, skv_s, wkv_s, feature_major(cache_cmp_kv[l]),
                                  feature_major(cache_sel_kv[l]), feature_major(cache_win_kv[l]), page_table, cmp_w,
                                  n_tok=n_tok)
    def token_major(a_t):
        n, _, r = a_t.shape
        return jnp.transpose(a_t.reshape(n, 2, NSA_GROUPS, NSA_HEAD_DIM, r), (0, 4, 1, 2, 3))[None]

    x1_s, h2_s, route_s = _mixout(x_sample.reshape(ts, d).astype(F32), yret_s, onsa_s, mg_s, mod_s[2], mod_s[3],
                                  mod_s[4], g_norm2[l], wr, wn, wo, w_rt, b_rt, tm=ts_tile, rows_per_mod=1)

    bm = 256
    h2 = jnp.concatenate([h2_p, h2_s], axis=0)
    route = jnp.concatenate([route_p, route_s], axis=0)
    expert = route[:, :EXPERT_TOPK].astype(I32)
    dest, blk_expert, row_tok, n_used = _dispatch_plan(expert, bm=bm)
    y_buf = _moe_experts(h2, blk_expert, row_tok, n_used,
                         w_e1[l].astype(BF16), w_e3[l].astype(BF16), w_e2[l].astype(BF16), bm=bm)
    dest_flat = dest.reshape(-1)
    y_p = _final(dest_flat, x1_p, route_p, mod_p[5], g_final, y_buf, tm=tm_p, rows_per_mod=seq // tm_p, row0=0)
    y_s = _final(dest_flat, x1_s, route_s, mod_s[5], g_final, y_buf, tm=ts_tile, rows_per_mod=1, row0=tp)

    kv6 = lambda a, n, s: a.reshape(1, n, s, 2, NSA_GROUPS, NSA_HEAD_DIM)
    return (y_p.reshape(batch, seq, d).astype(x_prompt.dtype), y_s.reshape(db, n_tok, d).astype(x_sample.dtype),
            ret_state_p[None].astype(state_ret.dtype), ret_state_s[None].astype(state_ret.dtype),
            token_major(ckv_pt).astype(cache_cmp_kv.dtype), kv6(ckv_s, db, n_tok).astype(cache_cmp_kv.dtype),
            token_major(skv_pt).astype(cache_sel_kv.dtype), kv6(skv_s, db, n_tok).astype(cache_sel_kv.dtype),
            token_major(wkv_pt[:, :, seq - win_buf:]).astype(cache_win_kv.dtype),
            token_major(win_s_t).astype(cache_win_kv.dtype))
```

```python
import functools

import numpy as np
import jax
import jax.numpy as jnp
from jax import lax
from jax.experimental import pallas as pl
from jax.experimental.pallas import tpu as pltpu

F32 = jnp.float32
BF16 = jnp.bfloat16
I32 = jnp.int32

D_MODEL = 1024
PAGE_SIZE = 128
RET_HEADS = 8
RET_DK = 64
RET_DV = 128
ROPE_BASE = 10000.0
NSA_HEADS = 16
NSA_GROUPS = 4
NSA_HPG = NSA_HEADS // NSA_GROUPS
NSA_HEAD_DIM = 64
CMP_LEN = 32
CMP_STRIDE = 16
CMP_HID = 128
SEL_BLOCK = 64
SEL_TOPK = 16
WINDOW = 512
N_GROUPS = 4
EXPERTS_PER_GROUP = 8
N_EXPERTS = N_GROUPS * EXPERTS_PER_GROUP
EXPERT_TOPK = 2
D_EXPERT = 512
RMS_EPS = 1e-6
LOG2E = 1.4426950408889634
NEG_INF = -1e30
FORCE_SCORE = 1e9

LANES = 128
ONES_ROWS = 16
D_RET_QK = RET_HEADS * RET_DK
D_RET_V = RET_HEADS * RET_DV
D_NSA = NSA_HEADS * NSA_HEAD_DIM
D_KV = 2 * NSA_GROUPS * NSA_HEAD_DIM
D_GK = NSA_GROUPS * NSA_HEAD_DIM
N_GATE = 3 * NSA_HEADS
C_RQ = 0
C_RK = C_RQ + D_RET_QK
C_RV = C_RK + D_RET_QK
C_RG = C_RV + D_RET_V
C_NQ = C_RG + D_RET_V
C_CKV = C_NQ + D_NSA
C_SKV = C_CKV + D_KV
C_WKV = C_SKV + D_KV
C_NG = C_WKV + D_KV
C_MG = C_NG + LANES
C_END = C_MG + 2 * D_MODEL
VMEM_LIMIT = 56 << 20


def _dot(a, b):
    return jnp.dot(a, b, preferred_element_type=F32)


def _dot_nt(a, b):
    return lax.dot_general(a, b, (((1,), (1,)), ((), ())), preferred_element_type=F32)


def _dot_tn(a, b):
    return lax.dot_general(a, b, (((0,), (0,)), ((), ())), preferred_element_type=F32)


def _split(x):
    hi = x.astype(BF16)
    return hi, (x - hi.astype(F32)).astype(BF16)


def _dot3(a, w):
    ah, al = _split(a)
    wh, wl = _split(w)
    return _dot(ah, wh) + _dot(ah, wl) + _dot(al, wh)


def _silu(x):
    return x * jax.nn.sigmoid(x)


def _params(*sem):
    return pltpu.CompilerParams(dimension_semantics=sem, vmem_limit_bytes=VMEM_LIMIT)


def _adaln_kernel(c_ref, w_ref, b_ref, o_ref):
    o_ref[...] = _dot3(_silu(c_ref[...]), w_ref[...]) + b_ref[...]


def _adaln(c, w, b):
    r, d = c.shape
    n = w.shape[1]
    tn = 1024
    return pl.pallas_call(
        _adaln_kernel,
        grid=(n // tn,),
        in_specs=[pl.BlockSpec((r, d), lambda j: (0, 0)),
                  pl.BlockSpec((d, tn), lambda j: (0, j)),
                  pl.BlockSpec((1, tn), lambda j: (0, j))],
        out_specs=pl.BlockSpec((r, tn), lambda j: (0, j)),
        out_shape=jax.ShapeDtypeStruct((r, n), F32),
        compiler_params=_params("arbitrary"),
        name="adaln",
    )(c, w, b.reshape(1, n))


def _rope(x, cos, sin):
    outs = []
    lane = lax.broadcasted_iota(I32, cos.shape, 1)
    first = (lane & 32) == 0
    for j in range(x.shape[1] // LANES):
        xj = x[:, j * LANES:(j + 1) * LANES]
        partner = jnp.where(first, pltpu.roll(xj, LANES - 32, 1), pltpu.roll(xj, 32, 1))
        outs.append(xj * cos + partner * sin)
    return jnp.concatenate(outs, axis=1)


def _inproj_kernel(x_ref, sh_ref, sc_ref, g_ref, cos_ref, sin_ref, w_ref, *rest, key_major):
    if key_major:
        (wt_ref, rq_ref, rk_ref, rv_ref, rg_ref, nq_ref, ckv_ref, skv_ref, wkv_ref, skb_ref, wkb_ref, ng_ref,
         mg_ref, svt_ref, wvt_ref) = rest
    else:
        (rq_ref, rk_ref, rv_ref, rg_ref, nq_ref, ckv_ref, skv_ref, wkv_ref, ng_ref, mg_ref) = rest
    x = x_ref[...]
    h = x * lax.rsqrt(jnp.mean(x * x, axis=-1, keepdims=True) + RMS_EPS) * g_ref[...]
    hb = (h * (1.0 + sc_ref[...]) + sh_ref[...]).astype(BF16)

    def proj(a, b):
        return _dot(hb, w_ref[:, a:b])

    cos = cos_ref[...]
    sin = sin_ref[...]
    rq_ref[...] = _rope(proj(C_RQ, C_RK), cos, sin).astype(rq_ref.dtype)
    rk_ref[...] = (_rope(proj(C_RK, C_RV), cos, sin) * (RET_DK ** -0.5)).astype(rk_ref.dtype)
    rv_ref[...] = proj(C_RV, C_RG).astype(rv_ref.dtype)
    rg_ref[...] = proj(C_RG, C_NQ).astype(rg_ref.dtype)
    if key_major:
        n_t = nq_ref.shape[0]

        def proj_t(a, b):
            return _dot_nt(wt_ref[a:b, :], hb)

        def tiles(o_ref, y):
            for k in range(n_t):
                o_ref[k] = y[:, k * LANES:(k + 1) * LANES].astype(o_ref.dtype)

        tiles(nq_ref, proj_t(0, D_NSA) * (NSA_HEAD_DIM ** -0.5 * LOG2E))
        ckv_ref[...] = proj_t(D_NSA, D_NSA + D_KV)
        skv = proj_t(D_NSA + D_KV, D_NSA + 2 * D_KV)
        skv_ref[...] = skv
        tiles(svt_ref, skv[D_GK:])
        wkv = proj_t(D_NSA + 2 * D_KV, D_NSA + 3 * D_KV)
        wkv_ref[...] = wkv
        tiles(wvt_ref, wkv[D_GK:])
        skb_ref[...] = proj(C_SKV, C_SKV + D_GK).astype(BF16)
        wkb_ref[...] = proj(C_WKV, C_WKV + D_GK).astype(BF16)
    else:
        nq_ref[...] = (proj(C_NQ, C_CKV) * (NSA_HEAD_DIM ** -0.5)).astype(nq_ref.dtype)
        ckv_ref[...] = proj(C_CKV, C_SKV)
        skv_ref[...] = proj(C_SKV, C_WKV)
        wkv_ref[...] = proj(C_WKV, C_NG)
    ng_ref[...] = jax.nn.sigmoid(proj(C_NG, C_MG))
    mg_ref[...] = jax.nn.sigmoid(proj(C_MG, C_END)).astype(BF16)


def _inproj(x, shift, scale, g, cos, sin, w_pad, w_t, *, tm, rows_per_mod, pos_blocks, act_dtype):
    t, d = x.shape
    r = shift.shape[1]
    key_major = w_t is not None
    mod_spec = pl.BlockSpec((None, r, d), lambda i: (i // rows_per_mod, 0, 0))
    pos_spec = pl.BlockSpec((tm, LANES), lambda i: (i % pos_blocks, 0))

    def out(cols, dt):
        return pl.BlockSpec((tm, cols), lambda i: (i, 0)), jax.ShapeDtypeStruct((t, cols), dt)

    def out_t(rows):
        return (pl.BlockSpec((tm // LANES, rows, LANES), lambda i: (i, 0, 0)),
                jax.ShapeDtypeStruct((t // LANES, rows, LANES), BF16))

    def out_fm():
        return (pl.BlockSpec((None, D_KV, tm), lambda i: (i // rows_per_mod, 0, i % rows_per_mod)),
                jax.ShapeDtypeStruct((t // (tm * rows_per_mod), D_KV, tm * rows_per_mod), F32))

    outs = [out(D_RET_QK, act_dtype), out(D_RET_QK, act_dtype), out(D_RET_V, act_dtype), out(D_RET_V, act_dtype)]
    in_specs = [pl.BlockSpec((tm, d), lambda i: (i, 0)), mod_spec, mod_spec,
                pl.BlockSpec((1, d), lambda i: (0, 0)), pos_spec, pos_spec,
                pl.BlockSpec((d, C_END), lambda i: (0, 0), pipeline_mode=pl.Buffered(1))]
    args = [x, shift, scale, g.reshape(1, d), cos, sin, w_pad]
    if key_major:
        outs += [out_t(D_NSA), out_fm(), out_fm(), out_fm(), out(D_GK, BF16), out(D_GK, BF16),
                 out(LANES, F32), out(2 * D_MODEL, BF16), out_t(D_GK), out_t(D_GK)]
        in_specs.append(pl.BlockSpec(w_t.shape, lambda i: (0, 0), pipeline_mode=pl.Buffered(1)))
        args.append(w_t)
    else:
        outs += [out(D_NSA, act_dtype), out(D_KV, F32), out(D_KV, F32), out(D_KV, F32),
                 out(LANES, F32), out(2 * D_MODEL, BF16)]
    return pl.pallas_call(
        functools.partial(_inproj_kernel, key_major=key_major),
        grid=(t // tm,),
        in_specs=in_specs,
        out_specs=[o[0] for o in outs],
        out_shape=[o[1] for o in outs],
        compiler_params=_params("arbitrary"),
        name="inproj",
    )(*args)


def _rope_tables(pos):
    half = RET_DK // 2
    inv_freq = ROPE_BASE ** (-jnp.arange(half, dtype=F32) / half)
    ang = pos.astype(F32)[:, None] * inv_freq[None, :]
    cos, sin = jnp.cos(ang), jnp.sin(ang)
    return jnp.tile(cos, (1, 4)), jnp.tile(jnp.concatenate([-sin, sin], axis=1), (1, 2))


def _pad_w_in(w_in):
    d = w_in.shape[0]
    n_real = C_NG + N_GATE
    return jnp.concatenate([w_in[:, :n_real], jnp.zeros((d, LANES - N_GATE), w_in.dtype), w_in[:, n_real:]],
                           axis=1).astype(BF16)


def _ret_log_decay():
    return np.log1p(-np.exp2(-5.0 - np.arange(RET_HEADS, dtype=np.float32))).astype(np.float32)


def _ret_head_out(o, g, gn):
    n = o * lax.rsqrt(jnp.mean(o * o, axis=-1, keepdims=True) + RMS_EPS) * gn
    return _silu(g) * n


def _ret_prompt_kernel(q_ref, k_ref, v_ref, g_ref, dm_ref, qd_ref, kd_ref, gn_ref, y_ref, sf_ref, s_scr, *, sdec):
    j = pl.program_id(1)

    @pl.when(j == 0)
    def _():
        s_scr[...] = jnp.zeros_like(s_scr)

    qb = q_ref[...]
    kb = k_ref[...]
    qd = (qb.astype(F32) * qd_ref[...]).astype(BF16)
    kd = (kb.astype(F32) * kd_ref[...]).astype(BF16)
    gn = gn_ref[...]
    for h in range(RET_HEADS):
        sl = slice(h * RET_DK, (h + 1) * RET_DK)
        vl = slice(h * RET_DV, (h + 1) * RET_DV)
        vh = v_ref[:, vl]
        s_h = s_scr[h]
        sc = _dot_nt(qb[:, sl], kb[:, sl]) * dm_ref[h]
        o = _dot(sc.astype(BF16), vh) + _dot(qd[:, sl], s_h.astype(BF16))
        s_scr[h] = sdec[h] * s_h + _dot_tn(kd[:, sl], vh)
        y_ref[:, vl] = _ret_head_out(o, g_ref[:, vl].astype(F32), gn).astype(y_ref.dtype)

    @pl.when(j == pl.num_programs(1) - 1)
    def _():
        sf_ref[...] = s_scr[...]


def _ret_tables(c):
    lg = _ret_log_decay()
    i = np.arange(c, dtype=np.float32)
    diff = i[:, None] - i[None, :]
    dm = np.where(diff[None] >= 0, np.exp(lg[:, None, None] * np.maximum(diff, 0.0)[None]), 0.0).astype(np.float32)
    qd = np.repeat(np.exp(lg[None, :] * (i[:, None] + 1.0)), RET_DK, axis=1).astype(np.float32)
    kd = np.repeat(np.exp(lg[None, :] * (c - 1.0 - i)[:, None]), RET_DK, axis=1).astype(np.float32)
    sdec = tuple(float(v) for v in np.exp(lg * np.float32(c)))
    return dm, qd, kd, sdec


def _ret_prompt(rq, rk, rv, rg, gn, *, batch, seq):
    c = 128
    n = seq // c
    dm, qd, kd, sdec = _ret_tables(c)
    row = lambda b, j: (b * n + j, 0)
    const2 = lambda b, j: (0, 0)
    return pl.pallas_call(
        functools.partial(_ret_prompt_kernel, sdec=sdec),
        grid=(batch, n),
        in_specs=[pl.BlockSpec((c, D_RET_QK), row), pl.BlockSpec((c, D_RET_QK), row),
                  pl.BlockSpec((c, D_RET_V), row), pl.BlockSpec((c, D_RET_V), row),
                  pl.BlockSpec((RET_HEADS, c, c), lambda b, j: (0, 0, 0)),
                  pl.BlockSpec((c, D_RET_QK), const2), pl.BlockSpec((c, D_RET_QK), const2),
                  pl.BlockSpec((1, RET_DV), const2)],
        out_specs=[pl.BlockSpec((c, D_RET_V), row),
                   pl.BlockSpec((None, RET_HEADS, RET_DK, RET_DV), lambda b, j: (b, 0, 0, 0))],
        out_shape=[jax.ShapeDtypeStruct((batch * seq, D_RET_V), BF16),
                   jax.ShapeDtypeStruct((batch, RET_HEADS, RET_DK, RET_DV), F32)],
        scratch_shapes=[pltpu.VMEM((RET_HEADS, RET_DK, RET_DV), F32)],
        compiler_params=_params("arbitrary", "arbitrary"),
        name="ret_prompt",
    )(rq, rk, rv, rg, jnp.asarray(dm), jnp.asarray(qd), jnp.asarray(kd), gn.reshape(1, RET_DV))


def _ret_sample_kernel(q_ref, k_ref, v_ref, g_ref, s0_ref, dm_ref, qd_ref, kd_ref, gn_ref, y_ref, sn_ref,
                       *, sdec, n_req, n_tok):
    rows = n_req * n_tok
    q = q_ref[...]
    k = k_ref[...]
    qd = q * qd_ref[...]
    kd = k * kd_ref[...]
    qb = q.astype(BF16)
    kb = k.astype(BF16)
    gn = gn_ref[...]
    req = lax.broadcasted_iota(I32, (rows, 1), 0) // n_tok
    for h in range(RET_HEADS):
        sl = slice(h * RET_DK, (h + 1) * RET_DK)
        vl = slice(h * RET_DV, (h + 1) * RET_DV)
        vh = v_ref[:, vl].astype(BF16)
        sc = _dot_nt(qb[:, sl], kb[:, sl]) * dm_ref[h]
        o = _dot(sc.astype(BF16), vh)
        qd_h = qd[:, sl]
        kd_h = kd[:, sl]
        for r in range(n_req):
            own = req == r
            s_r = s0_ref[r, h]
            o = o + _dot(jnp.where(own, qd_h, 0.0).astype(BF16), s_r.astype(BF16))
            sn_ref[r, h] = sdec[h] * s_r + _dot_tn(jnp.where(own, kd_h, 0.0).astype(BF16), vh)
        y_ref[:, vl] = _ret_head_out(o, g_ref[:, vl], gn).astype(y_ref.dtype)


def _ret_sample(rq, rk, rv, rg, state, gn, *, n_tok):
    db = state.shape[0]
    n_req = 16
    rows = n_req * n_tok
    lg = _ret_log_decay()
    i = np.arange(rows)
    t = (i % n_tok).astype(np.float32)
    same = (i[:, None] // n_tok) == (i[None, :] // n_tok)
    diff = t[:, None] - t[None, :]
    dm = np.where((same & (diff >= 0))[None], np.exp(lg[:, None, None] * np.maximum(diff, 0.0)[None]), 0.0)
    qd = np.repeat(np.exp(lg[None, :] * (t[:, None] + 1.0)), RET_DK, axis=1)
    kd = np.repeat(np.exp(lg[None, :] * (n_tok - 1.0 - t)[:, None]), RET_DK, axis=1)
    sdec = tuple(float(v) for v in np.exp(lg * np.float32(n_tok)))
    row = lambda i: (i, 0)
    const2 = lambda i: (0, 0)
    st_spec = pl.BlockSpec((n_req, RET_HEADS, RET_DK, RET_DV), lambda i: (i, 0, 0, 0))
    return pl.pallas_call(
        functools.partial(_ret_sample_kernel, sdec=sdec, n_req=n_req, n_tok=n_tok),
        grid=(db // n_req,),
        in_specs=[pl.BlockSpec((rows, D_RET_QK), row), pl.BlockSpec((rows, D_RET_QK), row),
                  pl.BlockSpec((rows, D_RET_V), row), pl.BlockSpec((rows, D_RET_V), row), st_spec,
                  pl.BlockSpec((RET_HEADS, rows, rows), lambda i: (0, 0, 0)),
                  pl.BlockSpec((rows, D_RET_QK), const2), pl.BlockSpec((rows, D_RET_QK), const2),
                  pl.BlockSpec((1, RET_DV), const2)],
        out_specs=[pl.BlockSpec((rows, D_RET_V), row), st_spec],
        out_shape=[jax.ShapeDtypeStruct((db * n_tok, D_RET_V), BF16),
                   jax.ShapeDtypeStruct(state.shape, F32)],
        compiler_params=_params("arbitrary"),
        name="ret_sample",
    )(rq, rk, rv, rg, state, jnp.asarray(dm, F32), jnp.asarray(qd, F32), jnp.asarray(kd, F32),
      gn.reshape(1, RET_DV))


def _cmp_weights(cmp_w1, cmp_pe, cmp_w2):
    w1c = jnp.concatenate([cmp_w1[:, :CMP_STRIDE], cmp_w1[:, CMP_STRIDE:]], axis=-1).astype(BF16)
    w1c = w1c.reshape(2, CMP_STRIDE // 4, 4 * NSA_HEAD_DIM, 2 * CMP_HID)
    eye = jnp.eye(NSA_GROUPS, dtype=cmp_w2.dtype)
    w2bd = jnp.einsum('gk,chd->cghkd', eye, cmp_w2).reshape(2, NSA_GROUPS * CMP_HID, D_GK).astype(BF16)
    pe8 = jnp.broadcast_to(cmp_pe.reshape(2, 1, CMP_LEN * NSA_HEAD_DIM), (2, 8, CMP_LEN * NSA_HEAD_DIM))
    w1f = cmp_w1.reshape(2, CMP_LEN * NSA_HEAD_DIM, CMP_HID)
    r = np.arange(PAGE_SIZE)
    seg_pp = PAGE_SIZE // CMP_STRIDE
    perm = r[None, :] == (r[:, None] % seg_pp) * CMP_STRIDE + r[:, None] // seg_pp
    return jnp.asarray(perm, BF16), w1c, w2bd, jnp.swapaxes(w2bd, 1, 2), pe8, w1f


def _cmp_pe_term(pe8_ref, w1f_ref, c):
    return _dot3(pe8_ref[c], w1f_ref[c])[0:1]


def _cmp_partial(pages, perm, w1c_ref, c):
    seg_pp = PAGE_SIZE // CMP_STRIDE
    ys = [_dot_nt(perm, pg[c * D_GK:(c + 1) * D_GK, :].astype(BF16)) for pg in pages]
    ysw = [jnp.concatenate([pltpu.roll(y[:, :LANES], NSA_HEAD_DIM, 1), pltpu.roll(y[:, LANES:], NSA_HEAD_DIM, 1)],
                           axis=1) for y in ys]
    lo_half = lax.broadcasted_iota(I32, (seg_pp * len(pages), LANES), 1) < NSA_HEAD_DIM

    def piece(src, p, col):
        return jnp.concatenate([y[p * seg_pp:(p + 1) * seg_pp, col * LANES:(col + 1) * LANES] for y in src], axis=0)

    acc = None
    for q in range(CMP_STRIDE // 4):
        rows = []
        for g in range(NSA_GROUPS):
            src_lo = ys if g % 2 == 0 else ysw
            src_hi = ysw if g % 2 == 0 else ys
            cols = [jnp.where(lo_half, piece(src_lo, 4 * q + 2 * j, g // 2), piece(src_hi, 4 * q + 2 * j + 1, g // 2))
                    for j in range(2)]
            rows.append(jnp.concatenate(cols, axis=1))
        d = _dot(jnp.concatenate(rows, axis=0).astype(BF16), w1c_ref[c, q])
        acc = d if acc is None else acc + d
    return acc


def _cmp_store(pages, perm, w1c_ref, lohi_scr, step):
    seg_ps = len(pages) * (PAGE_SIZE // CMP_STRIDE)
    off = pl.multiple_of(step * seg_ps, seg_ps)
    for c in range(2):
        acc = _cmp_partial(pages, perm, w1c_ref, c)
        for g in range(NSA_GROUPS):
            lohi_scr[c, g, pl.ds(off, seg_ps), :] = acc[g * seg_ps:(g + 1) * seg_ps]


def _cmp_finish(lohi, pe, w2_c, feature_major):
    hs = []
    for g in range(NSA_GROUPS):
        lh = lohi(g)
        nseg = lh.shape[0]
        z = lh[:, :CMP_HID] + pltpu.roll(lh[:, CMP_HID:], nseg - 1, 0) + pe
        hs.append(_silu(z).astype(BF16))
    hid = jnp.concatenate(hs, axis=1)
    if feature_major:
        return _dot_nt(w2_c, hid)
    return _dot(hid, w2_c)


def _q_blockdiag(qf, tq):
    lane = lax.broadcasted_iota(I32, (tq, D_GK), 1)
    blocks = []
    for g in range(NSA_GROUPS):
        slab = qf[:, g * D_GK:(g + 1) * D_GK]
        keep = (lane >= g * NSA_HEAD_DIM) & (lane < (g + 1) * NSA_HEAD_DIM)
        for r in range(NSA_HPG):
            sh = (NSA_HEAD_DIM * (g - r)) % D_GK
            rolled = slab if sh == 0 else pltpu.roll(slab, sh, 1)
            blocks.append(jnp.where(keep, rolled, 0.0))
    return jnp.concatenate(blocks, axis=0).astype(BF16)


def _pair_pv(p, v):
    half = p.shape[0] // 2
    return jnp.concatenate([_dot(p[:half], v[:, :LANES]), _dot(p[half:], v[:, LANES:])], axis=0)


def _assemble_heads(acc, tq):
    lane = lax.broadcasted_iota(I32, (tq, LANES), 1)
    chunks = []
    for c in range(NSA_HEADS // 2):
        a = acc[(2 * c) * tq:(2 * c + 1) * tq]
        b = acc[(2 * c + 1) * tq:(2 * c + 2) * tq]
        if (c // 2) % 2 == 0:
            b = pltpu.roll(b, NSA_HEAD_DIM, 1)
        else:
            a = pltpu.roll(a, NSA_HEAD_DIM, 1)
        chunks.append(jnp.where(lane < NSA_HEAD_DIM, a, b))
    return jnp.concatenate(chunks, axis=1)


def _gate_branches(ng, eg_ref, branches, tq):
    nh, nl = _split(ng)
    out = None
    for br, acc in enumerate(branches):
        gexp = _dot(nh, eg_ref[br]) + _dot(nl, eg_ref[br])
        term = gexp * _assemble_heads(acc, tq)
        out = term if out is None else out + term
    return out


def _masked_softmax(s, mask):
    s = jnp.where(mask, s, NEG_INF)
    m = jnp.max(s, axis=-1, keepdims=True)
    e = jnp.exp(s - m)
    return e / jnp.sum(e, axis=-1, keepdims=True)


def _rank_select(sc, sc_scr, valid, n_rows):
    sc_scr[...] = sc
    n_io = lax.broadcasted_iota(I32, sc.shape, 0)

    def body(m, cnt):
        row = sc_scr[pl.ds(m, 1), :]
        beats = (row > sc) | ((row == sc) & (m < n_io))
        return cnt + jnp.where(beats, 1, 0)

    cnt = lax.fori_loop(0, n_rows, body, jnp.zeros(sc.shape, I32))
    return jnp.where((cnt < SEL_TOPK) & valid, 1.0, 0.0)


def _topk_select(sc, valid, tri_ref):
    nb = sc.shape[0]
    bits = pltpu.bitcast(sc, I32)
    key = bits ^ ((bits >> 31) & jnp.int32(0x7FFFFFFF))

    def body(i, t):
        cand = t + (jnp.int32(1) << (31 - i))
        cnt = jnp.sum(jnp.where(key >= cand, 1.0, 0.0), axis=0, keepdims=True)
        return jnp.where(cnt >= SEL_TOPK, cand, t)

    t = lax.fori_loop(0, 32, body, jnp.full((1, sc.shape[1]), -2 ** 31, I32))
    gt = key > t
    eq = key == t
    need = SEL_TOPK - jnp.sum(jnp.where(gt, 1.0, 0.0), axis=0, keepdims=True)
    eq_pad = jnp.concatenate([jnp.where(eq, 1.0, 0.0), jnp.zeros((tri_ref.shape[1] - nb, sc.shape[1]), F32)], axis=0)
    before = _dot(tri_ref[...], eq_pad.astype(BF16))[:nb]
    return jnp.where((gt | (eq & (before < need))) & valid, 1.0, 0.0)


def _block_scores(scT, pos, n_io):
    valid = n_io * SEL_BLOCK <= pos
    cur = pos // SEL_BLOCK
    forced = (n_io == 0) | (n_io == cur) | (n_io == cur - 1)
    return jnp.where(forced, FORCE_SCORE, jnp.where(valid, scT, -FORCE_SCORE)), valid


def _nsa_tables(n_cmp_pad, n_cmp, n_blocks, nb_pad, n_keys_pad, tk, key_major=False):
    c = np.arange(n_cmp_pad)[None, :]
    n = np.arange(nb_pad)[:, None]
    ov = np.clip(np.minimum(CMP_STRIDE * c + CMP_LEN, SEL_BLOCK * n + SEL_BLOCK)
                 - np.maximum(CMP_STRIDE * c, SEL_BLOCK * n), 0, None) / CMP_STRIDE
    ov = np.where((c < n_cmp) & (n < n_blocks), ov, 0.0)
    key = np.arange(n_keys_pad)
    e = (key[None, :] // SEL_BLOCK == np.arange(LANES * ((nb_pad + LANES - 1) // LANES))[:, None])
    e3 = e.reshape(e.shape[0], n_keys_pad // tk, tk).transpose(1, 0, 2)
    if key_major:
        e3 = e3.transpose(0, 2, 1)
    col = np.arange(LANES)[:, None]
    lane = np.arange(D_NSA)[None, :]
    eg = np.stack([col == br * NSA_HEADS + lane // NSA_HEAD_DIM for br in range(3)])
    return jnp.asarray(ov, BF16), jnp.asarray(e3, BF16), jnp.asarray(eg, BF16)


def _cmp_prompt_kernel(ck_ref, perm_ref, w1c_ref, w2_ref, w2t_ref, pe8_ref, w1f_ref, ko_ref, vto_ref, lohi_scr):
    st = pl.program_id(1)
    n_pg = ck_ref.shape[1] // PAGE_SIZE
    pages = [ck_ref[:, k * PAGE_SIZE:(k + 1) * PAGE_SIZE] for k in range(n_pg)]
    _cmp_store(pages, perm_ref[...], w1c_ref, lohi_scr, st)

    @pl.when(st == pl.num_programs(1) - 1)
    def _():
        ko_ref[...] = _cmp_finish(lambda g: lohi_scr[0, g], _cmp_pe_term(pe8_ref, w1f_ref, 0), w2_ref[0],
                                  False).astype(ko_ref.dtype)
        vto_ref[...] = _cmp_finish(lambda g: lohi_scr[1, g], _cmp_pe_term(pe8_ref, w1f_ref, 1), w2t_ref[1],
                                   True).astype(vto_ref.dtype)


def _cmp_prompt(ckv_t, cmp_w, *, batch, seq):
    nseg = seq // CMP_STRIDE
    rows = 8 * PAGE_SIZE
    n_st = seq // rows
    full = lambda a: pl.BlockSpec(a.shape, lambda b, s: (0,) * a.ndim)
    return pl.pallas_call(
        _cmp_prompt_kernel,
        grid=(batch, n_st),
        in_specs=[pl.BlockSpec((None, D_KV, rows), lambda b, s: (b, 0, s))] + [full(a) for a in cmp_w],
        out_specs=[pl.BlockSpec((nseg, D_GK), lambda b, s: (b, 0)),
                   pl.BlockSpec((None, D_GK, nseg), lambda b, s: (b, 0, 0))],
        out_shape=[jax.ShapeDtypeStruct((batch * nseg, D_GK), BF16),
                   jax.ShapeDtypeStruct((batch, D_GK, nseg), BF16)],
        scratch_shapes=[pltpu.VMEM((2, NSA_GROUPS, nseg, 2 * CMP_HID), F32)],
        compiler_params=_params("arbitrary", "arbitrary"),
        name="cmp_prompt",
    )(ckv_t, *cmp_w)


def _col_softmax(s, mask):
    s = jnp.where(mask, s, NEG_INF)
    return jnp.exp2(s - jnp.max(s, axis=0, keepdims=True))


def _nsa_prompt_kernel(qt_ref, ng_ref, ck_ref, cvt_ref, sk_ref, svt_ref, wk_ref, wvt_ref, ov_ref, et_ref,
                       o_ref, sc_scr, *gscr, tq, tk, wlen):
    seq = sk_ref.shape[0]
    nseg = ck_ref.shape[0]
    nb = ov_ref.shape[0]
    gl = NSA_HPG * tq
    hd = NSA_HEAD_DIM
    q0 = pl.program_id(1) * tq

    qt = qt_ref[0]
    qg = []
    for g in range(NSA_GROUPS):
        cols = []
        for r in range(NSA_HPG):
            h = NSA_HPG * g + r
            parts = [qt[h * hd:(h + 1) * hd, :]]
            if g > 0:
                parts.insert(0, jnp.zeros((hd * g, tq), BF16))
            if g < NSA_GROUPS - 1:
                parts.append(jnp.zeros((hd * (NSA_GROUPS - 1 - g), tq), BF16))
            cols.append(jnp.concatenate(parts, axis=0))
        qg.append(jnp.concatenate(cols, axis=1))
    pos_t = q0 + lax.broadcasted_iota(I32, (1, gl), 1) % tq

    ck = ck_ref[...]
    cidx = lax.broadcasted_iota(I32, (nseg, gl), 0)
    cmask = (CMP_STRIDE * cidx + (CMP_LEN - 1) <= pos_t) & (cidx < nseg - 1)
    o_cmp, pgs = [], []
    ss = [_dot(ck, qg[g]) for g in range(NSA_GROUPS)]
    for g in range(NSA_GROUPS):
        e = _col_softmax(ss[g], cmask)
        p = jnp.where(cmask, e * (1.0 / jnp.sum(e, axis=0, keepdims=True)), 0.0)
        pgs.append(p[:, 0:tq] + p[:, tq:2 * tq] + p[:, 2 * tq:3 * tq] + p[:, 3 * tq:4 * tq])
        ss[g] = p.astype(BF16)
    for g in range(NSA_GROUPS):
        o_cmp.append(_dot(cvt_ref[g * hd:(g + 1) * hd, :], ss[g]))
    ph, plo = _split(jnp.concatenate(pgs, axis=1))
    sc_t = _dot(ov_ref[...], ph) + _dot(ov_ref[...], plo)

    n_io = lax.broadcasted_iota(I32, sc_t.shape, 0)
    pos_l = q0 + lax.broadcasted_iota(I32, sc_t.shape, 1) % tq
    sc, valid = _block_scores(sc_t, pos_l, n_io)
    sel_t = _rank_select(sc, sc_scr, valid, jnp.minimum((q0 + tq - 1) // SEL_BLOCK + 1, nb))
    selb = jnp.concatenate([sel_t, jnp.zeros((LANES - nb, sc_t.shape[1]), F32)], axis=0).astype(BF16)

    m_scr, acc_scr = gscr[:NSA_GROUPS], gscr[NSA_GROUPS:]
    for g in range(NSA_GROUPS):
        m_scr[g][...] = jnp.full(m_scr[g].shape, NEG_INF, F32)
        acc_scr[g][...] = jnp.zeros(acc_scr[g].shape, F32)
    key_i = lax.broadcasted_iota(I32, (tk, tq), 0)
    pos_q = q0 + lax.broadcasted_iota(I32, (tk, tq), 1)
    ones_k = jnp.ones((ONES_ROWS, tk), BF16)

    def kv_step(j, carry, on_diagonal):
        k0 = pl.multiple_of(j * tk, tk)
        kt = sk_ref[pl.ds(k0, tk), :]
        selm = _dot(et_ref[j], selb)
        ss, ms = [], []
        for g in range(NSA_GROUPS):
            ok = selm[:, g * tq:(g + 1) * tq] > 0.5
            if on_diagonal:
                ok = ok & (k0 + key_i <= pos_q)
            bias = jnp.where(ok, 0.0, NEG_INF)
            s = _dot(kt, qg[g]) + jnp.concatenate([bias] * NSA_HPG, axis=1)
            ss.append(s)
            ms.append(jnp.maximum(m_scr[g][...], jnp.max(s, axis=0, keepdims=True)))
        ps = [jnp.exp2(ss[g] - ms[g]).astype(BF16) for g in range(NSA_GROUPS)]
        for g in range(NSA_GROUPS):
            vt = jnp.concatenate([svt_ref[j * (tk // LANES) + u, g * hd:(g + 1) * hd, :]
                                  for u in range(tk // LANES)], axis=1)
            alpha = jnp.exp2(m_scr[g][...] - ms[g])
            acc_scr[g][...] = alpha * acc_scr[g][...] + _dot(jnp.concatenate([vt, ones_k], axis=0), ps[g])
            m_scr[g][...] = ms[g]
        return carry

    n_before = q0 // tk
    lax.fori_loop(0, n_before, functools.partial(kv_step, on_diagonal=False), 0)
    lax.fori_loop(n_before, (q0 + tq + tk - 1) // tk, functools.partial(kv_step, on_diagonal=True), 0)

    w0 = pl.multiple_of(jnp.clip(q0 + tq - wlen, 0, seq - wlen), LANES)
    kw = wk_ref[pl.ds(w0, wlen), :]
    dist = pos_t - (w0 + lax.broadcasted_iota(I32, (wlen, gl), 0))
    wmask = (dist >= 0) & (dist < WINDOW)
    o_win = []
    ones_w = jnp.ones((ONES_ROWS, wlen), BF16)
    ss = [_dot(kw, qg[g]) for g in range(NSA_GROUPS)]
    ss = [_col_softmax(ss[g], wmask).astype(BF16) for g in range(NSA_GROUPS)]
    for g in range(NSA_GROUPS):
        vt = jnp.concatenate([wvt_ref[w0 // LANES + u, g * hd:(g + 1) * hd, :] for u in range(wlen // LANES)], axis=1)
        ow = _dot(jnp.concatenate([vt, ones_w], axis=0), ss[g])
        o_win.append(ow[:hd] * (1.0 / ow[hd:hd + 1]))

    ngt = ng_ref[...].T
    heads = []
    for g in range(NSA_GROUPS):
        acc = acc_scr[g][...]
        o_sel = acc[:hd] * (1.0 / acc[hd:hd + 1])
        for r in range(NSA_HPG):
            h = NSA_HPG * g + r
            lanes = slice(r * tq, (r + 1) * tq)
            heads.append(ngt[h:h + 1, :] * o_cmp[g][:, lanes]
                         + ngt[NSA_HEADS + h:NSA_HEADS + h + 1, :] * o_sel[:, lanes]
                         + ngt[2 * NSA_HEADS + h:2 * NSA_HEADS + h + 1, :] * o_win[g][:, lanes])
    ot = jnp.concatenate(heads, axis=0)
    o_ref[...] = jnp.concatenate([ot[c * LANES:(c + 1) * LANES].T for c in range(D_NSA // LANES)],
                                 axis=1).astype(o_ref.dtype)


def _nsa_prompt(nqt, ng, ck, cvt, skb, svt, wkb, wvt, *, batch, seq):
    tq, tk, wlen = LANES, 256, 640
    assert seq % tk == 0 and seq >= wlen and wlen >= WINDOW + tq
    nseg = seq // CMP_STRIDE
    n_blocks = seq // SEL_BLOCK
    nq_t = seq // tq
    ov, et, _ = _nsa_tables(nseg, nseg - 1, n_blocks, n_blocks, seq, tk, key_major=True)
    gl = NSA_HPG * tq
    qrow = lambda b, i: (b * nq_t + i, 0)
    full = lambda a: pl.BlockSpec(a.shape, lambda b, i: (0,) * a.ndim)
    kspec = pl.BlockSpec((seq, D_GK), lambda b, i: (b, 0))
    vtspec = pl.BlockSpec((seq // LANES, D_GK, LANES), lambda b, i: (b, 0, 0))
    return pl.pallas_call(
        functools.partial(_nsa_prompt_kernel, tq=tq, tk=tk, wlen=wlen),
        grid=(batch, nq_t),
        in_specs=[pl.BlockSpec((1, D_NSA, tq), lambda b, i: (b * nq_t + i, 0, 0)), pl.BlockSpec((tq, LANES), qrow),
                  pl.BlockSpec((nseg, D_GK), lambda b, i: (b, 0)),
                  pl.BlockSpec((None, D_GK, nseg), lambda b, i: (b, 0, 0)),
                  kspec, vtspec, kspec, vtspec, full(ov), full(et)],
        out_specs=pl.BlockSpec((tq, D_NSA), qrow),
        out_shape=jax.ShapeDtypeStruct((batch * seq, D_NSA), BF16),
        scratch_shapes=[pltpu.VMEM((n_blocks, NSA_GROUPS * tq), F32)]
        + [pltpu.VMEM((1, gl), F32)] * NSA_GROUPS
        + [pltpu.VMEM((NSA_HEAD_DIM + ONES_ROWS, gl), F32)] * NSA_GROUPS,
        compiler_params=_params("arbitrary", "arbitrary"),
        name="nsa_prompt",
    )(nqt, ng, ck, cvt, skb, svt, wkb, wvt, ov, et)


def _nsa_sample_kernel(pt_ref, *refs, n_tok, ppb, n_blocks, past):
    cmp_refs = refs[:ppb]
    sel_refs = refs[ppb:2 * ppb]
    (q_ref, ng_ref, snew_ref, wnew_ref, wint_ref, perm_ref, w1c_ref, w2_ref, w2t_ref, pe8_ref, w1f_ref, ov_ref,
     e_ref, eg_ref, tri_ref, o_ref, wout_ref, lohi_scr, sel_scr, s_scr, pe_scr) = refs[2 * ppb:]
    del pt_ref
    b = pl.program_id(0)
    st = pl.program_id(1)
    n_st = sel_scr.shape[0]
    nseg = lohi_scr.shape[2]
    tq = n_tok
    rows = NSA_HEADS * tq
    tkp = ppb * PAGE_SIZE

    @pl.when((b == 0) & (st == 0))
    def _():
        for c in range(2):
            pe_scr[c] = jnp.broadcast_to(_cmp_pe_term(pe8_ref, w1f_ref, c), (8, CMP_HID))

    _cmp_store([r[...] for r in cmp_refs], perm_ref[...], w1c_ref, lohi_scr, st)
    for k, r in enumerate(sel_refs):
        sel_scr[st, :, k * PAGE_SIZE:(k + 1) * PAGE_SIZE] = r[...].astype(BF16)

    @pl.when(st == n_st - 1)
    def _():
        qbd = _q_blockdiag(q_ref[...], tq)
        pos_r = past + lax.broadcasted_iota(I32, (rows, 1), 0) % tq
        pad = jnp.zeros((LANES - tq, D_KV), F32)
        snew_t = jnp.concatenate([snew_ref[...], pad], axis=0).T.astype(BF16)
        wnew_t = jnp.concatenate([wnew_ref[...], pad], axis=0).T

        kct = _cmp_finish(lambda g: lohi_scr[0, g], pe_scr[0, 0:1], w2t_ref[0], True).astype(BF16)
        vc = _cmp_finish(lambda g: lohi_scr[1, g], pe_scr[1, 0:1], w2_ref[1], False).astype(BF16)
        s = _dot(qbd, kct)
        cidx = lax.broadcasted_iota(I32, (rows, nseg), 1)
        cmask = (CMP_STRIDE * cidx + (CMP_LEN - 1) <= pos_r) & (cidx < nseg - 1)
        p = jnp.where(cmask, _masked_softmax(s, cmask), 0.0)
        o_cmp = _pair_pv(p.astype(BF16), vc)
        pg = jnp.sum(p.reshape(NSA_GROUPS, NSA_HPG, tq, nseg), axis=1, keepdims=True)
        pg = jnp.broadcast_to(pg, (NSA_GROUPS, NSA_HPG, tq, nseg)).reshape(rows, nseg)
        ph, plo = _split(pg)
        sc_t = _dot_nt(ov_ref[...], ph) + _dot_nt(ov_ref[...], plo)

        nb_pad = ov_ref.shape[0]
        nb8 = -(-n_blocks // 8) * 8
        n_io = lax.broadcasted_iota(I32, (nb8, rows), 0)
        pos_l = past + lax.broadcasted_iota(I32, (nb8, rows), 1) % tq
        sc, valid = _block_scores(sc_t[:nb8], pos_l, n_io)
        sel_t = _topk_select(sc, valid & (n_io < n_blocks), tri_ref)
        sel = jnp.concatenate([sel_t, jnp.zeros((nb_pad - nb8, rows), F32)], axis=0).T.astype(BF16)

        for t in range(n_st):
            selm = _dot(sel, e_ref[t])
            s_scr[:, t * tkp:(t + 1) * tkp] = jnp.where(selm > 0.5, _dot(qbd, sel_scr[t, 0:D_GK, :]), NEG_INF)
        selm = _dot(sel, e_ref[n_st, :, 0:LANES])
        new_ok = (selm > 0.5) & (past + lax.broadcasted_iota(I32, (rows, LANES), 1) <= pos_r)
        s_scr[:, n_st * tkp:] = jnp.where(new_ok, _dot(qbd, snew_t[0:D_GK]), NEG_INF)
        s_all = s_scr[...]
        ps = jnp.exp(s_all - jnp.max(s_all, axis=-1, keepdims=True))
        l_sel = jnp.sum(ps, axis=-1, keepdims=True)
        ps = ps.astype(BF16)
        acc = _dot_nt(ps[:, n_st * tkp:], snew_t[D_GK:])
        for t in range(n_st):
            acc = acc + _dot_nt(ps[:, t * tkp:(t + 1) * tkp], sel_scr[t, D_GK:, :])
        acc = acc * (1.0 / l_sel)
        half = rows // 2
        o_sel = jnp.concatenate([acc[:half, :LANES], acc[half:, LANES:]], axis=0)

        win_buf = wint_ref.shape[1]
        wt = wint_ref[...]
        wtb = wt.astype(BF16)
        wnb = wnew_t.astype(BF16)
        sw = jnp.concatenate([_dot(qbd, wtb[0:D_GK]), _dot(qbd, wnb[0:D_GK])], axis=1)
        k_pos = past - win_buf + lax.broadcasted_iota(I32, sw.shape, 1)
        dist = pos_r - k_pos
        pw = _masked_softmax(sw, (dist >= 0) & (dist < WINDOW) & (k_pos >= 0)).astype(BF16)
        accw = _dot_nt(pw[:, :win_buf], wtb[D_GK:]) + _dot_nt(pw[:, win_buf:], wnb[D_GK:])
        o_win = jnp.concatenate([accw[:half, :LANES], accw[half:, LANES:]], axis=0)

        o_ref[...] = _gate_branches(ng_ref[...], eg_ref, (o_cmp, o_sel, o_win), tq)

        shifted = pltpu.roll(wt, win_buf - tq, 1)
        newr = pltpu.roll(wnew_t, LANES - tq, 1)
        lane = lax.broadcasted_iota(I32, (D_KV, LANES), 1)
        wout_ref[:, 0:win_buf - LANES] = shifted[:, 0:win_buf - LANES]
        wout_ref[:, win_buf - LANES:] = jnp.where(lane >= LANES - tq, newr, shifted[:, win_buf - LANES:])


def _nsa_sample(nq, ng, skv, wkv, cache_cmp_t, cache_sel_t, cache_win_t, page_table, cmp_w, *, n_tok):
    db, n_pages = page_table.shape
    past = n_pages * PAGE_SIZE
    win_buf = cache_win_t.shape[2]
    ppb = 8
    tkp = ppb * PAGE_SIZE
    assert n_tok == 8 and n_tok < CMP_STRIDE and n_pages % ppb == 0 and win_buf % LANES == 0 and win_buf >= LANES
    n_st = n_pages // ppb
    nseg = past // CMP_STRIDE
    n_blocks = -(-(past + n_tok) // SEL_BLOCK)
    nb_pad = -(-n_blocks // LANES) * LANES
    ov, e3, eg = _nsa_tables(nseg, nseg - 1, n_blocks, nb_pad, (n_st + 1) * tkp, tkp)
    tri = jnp.asarray(np.arange(nb_pad)[None, :] < np.arange(nb_pad)[:, None], BF16)
    rows = NSA_HEADS * n_tok

    def page_spec(k):
        return pl.BlockSpec((None, D_KV, PAGE_SIZE), lambda b, s, pt: (pt[b, s * ppb + k], 0, 0))

    req = lambda cols: pl.BlockSpec((n_tok, cols), lambda b, s, pt: (b, 0))
    full = lambda a: pl.BlockSpec(a.shape, lambda b, s, pt: (0,) * a.ndim)
    win_spec = pl.BlockSpec((None, D_KV, win_buf), lambda b, s, pt: (b, 0, 0))
    grid_spec = pltpu.PrefetchScalarGridSpec(
        num_scalar_prefetch=1,
        grid=(db, n_st),
        in_specs=[page_spec(k) for k in range(ppb)] + [page_spec(k) for k in range(ppb)]
        + [req(D_NSA), req(LANES), req(D_KV), req(D_KV), win_spec]
        + [full(a) for a in cmp_w]
        + [full(ov), pl.BlockSpec(e3.shape, lambda b, s, pt: (0, 0, 0), pipeline_mode=pl.Buffered(1)), full(eg),
           full(tri)],
        out_specs=[req(D_NSA), win_spec],
        scratch_shapes=[pltpu.VMEM((2, NSA_GROUPS, nseg, 2 * CMP_HID), F32),
                        pltpu.VMEM((n_st, D_KV, tkp), BF16),
                        pltpu.VMEM((rows, n_st * tkp + LANES), F32),
                        pltpu.VMEM((2, 8, CMP_HID), F32)],
    )
    return pl.pallas_call(
        functools.partial(_nsa_sample_kernel, n_tok=n_tok, ppb=ppb, n_blocks=n_blocks, past=past),
        grid_spec=grid_spec,
        out_shape=[jax.ShapeDtypeStruct((db * n_tok, D_NSA), F32),
                   jax.ShapeDtypeStruct((db, D_KV, win_buf), F32)],
        compiler_params=_params("arbitrary", "arbitrary"),
        name="nsa_sample",
    )(page_table, *([cache_cmp_t] * ppb), *([cache_sel_t] * ppb), nq, ng, skv, wkv, cache_win_t,
      *cmp_w, ov, e3, eg, tri)


def _route(lg):
    lane = lax.broadcasted_iota(I32, lg.shape, 1).astype(F32)
    big = float(1 << 20)
    is_g = lane < N_GROUPS
    gl = jnp.where(is_g, lg, -jnp.inf)
    gmax = jnp.max(gl, axis=-1, keepdims=True)
    g_star = jnp.min(jnp.where(gl == gmax, lane, big), axis=-1, keepdims=True)
    g_w = 1.0 / jnp.sum(jnp.where(is_g, jnp.exp(lg - gmax), 0.0), axis=-1, keepdims=True)
    lo = N_GROUPS + g_star * EXPERTS_PER_GROUP
    el = jnp.where((lane >= lo) & (lane < lo + EXPERTS_PER_GROUP), lg, -jnp.inf)
    m1 = jnp.max(el, axis=-1, keepdims=True)
    i1 = jnp.min(jnp.where(el == m1, lane, big), axis=-1, keepdims=True)
    el2 = jnp.where(lane == i1, -jnp.inf, el)
    m2 = jnp.max(el2, axis=-1, keepdims=True)
    i2 = jnp.min(jnp.where(el2 == m2, lane, big), axis=-1, keepdims=True)
    e2 = jnp.exp(m2 - m1)
    p1 = 1.0 / (1.0 + e2)
    out = jnp.where(lane == 0, i1 - N_GROUPS, 0.0)
    out = jnp.where(lane == 1, i2 - N_GROUPS, out)
    out = jnp.where(lane == 2, g_w * p1, out)
    return jnp.where(lane == 3, g_w * (e2 * p1), out)


def _mixout_kernel(x_ref, yr_ref, on_ref, mg_ref, gt_ref, sh_ref, sc_ref, g2_ref, wr_ref, wn_ref, wo_ref,
                   wrt_ref, brt_ref, x1_ref, h2_ref, rt_ref):
    y_ret = _dot(yr_ref[...], wr_ref[...])
    y_nsa = _dot(on_ref[...].astype(BF16), wn_ref[...])
    mg = mg_ref[...].astype(F32)
    z = mg[:, :D_MODEL] * y_ret + mg[:, D_MODEL:] * y_nsa
    x1 = x_ref[...] + gt_ref[...] * _dot(z.astype(BF16), wo_ref[...])
    x1_ref[...] = x1
    h = x1 * lax.rsqrt(jnp.mean(x1 * x1, axis=-1, keepdims=True) + RMS_EPS) * g2_ref[...]
    h2 = h * (1.0 + sc_ref[...]) + sh_ref[...]
    h2_ref[...] = h2
    rt_ref[...] = _route(_dot3(h2, wrt_ref[...]) + brt_ref[...])


def _mixout(x, yret, onsa, mg, gate, shift, scale, g2, w_ret_o, w_nsa_o, w_out, w_rt, b_rt, *, tm, rows_per_mod):
    t, d = x.shape
    r = gate.shape[1]
    row = lambda cols: pl.BlockSpec((tm, cols), lambda i: (i, 0))
    mod_spec = pl.BlockSpec((None, r, d), lambda i: (i // rows_per_mod, 0, 0))
    full = lambda a: pl.BlockSpec(a.shape, lambda i: (0,) * a.ndim)
    return pl.pallas_call(
        _mixout_kernel,
        grid=(t // tm,),
        in_specs=[row(d), row(D_RET_V), row(D_NSA), row(2 * d), mod_spec, mod_spec, mod_spec,
                  pl.BlockSpec((1, d), lambda i: (0, 0)), full(w_ret_o), full(w_nsa_o), full(w_out),
                  full(w_rt), full(b_rt)],
        out_specs=[row(d), row(d), row(LANES)],
        out_shape=[jax.ShapeDtypeStruct((t, d), F32), jax.ShapeDtypeStruct((t, d), F32),
                   jax.ShapeDtypeStruct((t, LANES), F32)],
        compiler_params=_params("arbitrary"),
        name="mixout",
    )(x, yret, onsa, mg, gate, shift, scale, g2.reshape(1, d), w_ret_o, w_nsa_o, w_out, w_rt, b_rt)


def _moe_kernel(be_ref, rt_ref, nu_ref, x_hbm, w1_ref, w3_ref, w2_ref, y_ref, xbuf, sem, *, bm):
    i = pl.program_id(0)
    n_used = nu_ref[0]

    def start_rows(blk, slot):
        for r in range(bm):
            tok = rt_ref[blk * bm + r]
            pltpu.make_async_copy(x_hbm.at[pl.ds(tok, 1), :], xbuf.at[slot, pl.ds(r, 1), :], sem.at[slot]).start()

    def wait_rows(slot):
        pltpu.make_async_copy(x_hbm.at[pl.ds(0, bm), :], xbuf.at[slot], sem.at[slot]).wait()

    @pl.when((i == 0) & (n_used > 0))
    def _():
        start_rows(0, 0)

    @pl.when(i < n_used)
    def _():
        slot = i % 2
        wait_rows(slot)
        xb = xbuf[slot].astype(BF16)
        start_rows(jnp.minimum(i + 1, n_used - 1), 1 - slot)
        hid = _silu(_dot(xb, w1_ref[...])) * _dot(xb, w3_ref[...])
        y_ref[...] = _dot(hid.astype(BF16), w2_ref[...])

    @pl.when(i == n_used - 1)
    def _():
        wait_rows(1 - i % 2)

    @pl.when(i >= n_used)
    def _():
        y_ref[...] = jnp.zeros(y_ref.shape, F32)


def _moe_experts(h2, blk_expert, row_tok, n_used, w_e1, w_e3, w_e2, *, bm):
    t, d = h2.shape
    n_blk = blk_expert.shape[0]
    de = w_e1.shape[-1]
    grid_spec = pltpu.PrefetchScalarGridSpec(
        num_scalar_prefetch=3,
        grid=(n_blk,),
        in_specs=[pl.BlockSpec(memory_space=pl.ANY),
                  pl.BlockSpec((None, d, de), lambda i, be, rt, nu: (be[i], 0, 0)),
                  pl.BlockSpec((None, d, de), lambda i, be, rt, nu: (be[i], 0, 0)),
                  pl.BlockSpec((None, de, d), lambda i, be, rt, nu: (be[i], 0, 0))],
        out_specs=pl.BlockSpec((bm, d), lambda i, be, rt, nu: (i, 0)),
        scratch_shapes=[pltpu.VMEM((2, bm, d), F32), pltpu.SemaphoreType.DMA((2,))],
    )
    return pl.pallas_call(
        functools.partial(_moe_kernel, bm=bm),
        grid_spec=grid_spec,
        out_shape=jax.ShapeDtypeStruct((n_blk * bm, d), F32),
        compiler_params=_params("arbitrary"),
        name="moe_experts",
    )(blk_expert, row_tok, n_used, h2, w_e1, w_e3, w_e2)


def _dispatch_plan(expert, *, bm):
    t = expert.shape[0]
    tk = t * EXPERT_TOPK
    e_flat = expert.reshape(-1)
    onehot = (e_flat[:, None] == jnp.arange(N_EXPERTS, dtype=I32)[None, :]).astype(I32)
    csum = jnp.cumsum(onehot, axis=0)
    counts = csum[-1]
    rank = jnp.sum((csum - onehot) * onehot, axis=1)
    padded = (counts + bm - 1) // bm * bm
    pad_end = jnp.cumsum(padded)
    pad_start = pad_end - padded
    dest = pad_start[e_flat] + rank
    n_blk = -(-tk // bm) + N_EXPERTS
    blk_start = jnp.arange(n_blk, dtype=I32) * bm
    blk_expert = jnp.minimum(jnp.sum((pad_end[None, :] <= blk_start[:, None]).astype(I32), axis=1), N_EXPERTS - 1)
    row_tok = jnp.zeros((n_blk * bm,), I32).at[dest].set(jnp.arange(tk, dtype=I32) // EXPERT_TOPK)
    n_used = (pad_end[-1:] // bm).astype(I32)
    return dest.reshape(t, EXPERT_TOPK), blk_expert, row_tok, n_used


def _final_kernel(ds_ref, x1_ref, rt_ref, gt_ref, gf_ref, y_hbm, o_ref, ybuf, sem, *, tm, row0):
    i = pl.program_id(0)
    n = pl.num_programs(0)

    def start_rows(blk, slot):
        for r in range(tm):
            for k in range(EXPERT_TOPK):
                src = ds_ref[(row0 + blk * tm + r) * EXPERT_TOPK + k]
                pltpu.make_async_copy(y_hbm.at[pl.ds(src, 1), :], ybuf.at[slot, k, pl.ds(r, 1), :],
                                      sem.at[slot]).start()

    def wait_rows(slot):
        for k in range(EXPERT_TOPK):
            pltpu.make_async_copy(y_hbm.at[pl.ds(0, tm), :], ybuf.at[slot, k], sem.at[slot]).wait()

    @pl.when(i == 0)
    def _():
        start_rows(0, 0)

    slot = i % 2
    wait_rows(slot)
    y0 = ybuf[slot, 0]
    y1 = ybuf[slot, 1]
    start_rows(jnp.minimum(i + 1, n - 1), 1 - slot)
    rt = rt_ref[...]
    x2 = x1_ref[...] + gt_ref[...] * (rt[:, 2:3] * y0 + rt[:, 3:4] * y1)
    o_ref[...] = x2 * lax.rsqrt(jnp.mean(x2 * x2, axis=-1, keepdims=True) + RMS_EPS) * gf_ref[...]

    @pl.when(i == n - 1)
    def _():
        wait_rows(1 - slot)


def _final(dest_flat, x1, route, gate, g_final, y_buf, *, tm, rows_per_mod, row0):
    t, d = x1.shape
    r = gate.shape[1]
    grid_spec = pltpu.PrefetchScalarGridSpec(
        num_scalar_prefetch=1,
        grid=(t // tm,),
        in_specs=[pl.BlockSpec((tm, d), lambda i, ds: (i, 0)), pl.BlockSpec((tm, LANES), lambda i, ds: (i, 0)),
                  pl.BlockSpec((None, r, d), lambda i, ds: (i // rows_per_mod, 0, 0)),
                  pl.BlockSpec((1, d), lambda i, ds: (0, 0)), pl.BlockSpec(memory_space=pl.ANY)],
        out_specs=pl.BlockSpec((tm, d), lambda i, ds: (i, 0)),
        scratch_shapes=[pltpu.VMEM((2, EXPERT_TOPK, tm, d), F32), pltpu.SemaphoreType.DMA((2,))],
    )
    return pl.pallas_call(
        functools.partial(_final_kernel, tm=tm, row0=row0),
        grid_spec=grid_spec,
        out_shape=jax.ShapeDtypeStruct((t, d), F32),
        compiler_params=_params("arbitrary"),
        name="final",
    )(dest_flat, x1, route, gate, g_final.reshape(1, d), y_buf)


def kernel(x_prompt, x_sample, c_prompt, c_sample, state_ret, cache_cmp_kv, cache_sel_kv, cache_win_kv, page_table,
           w_ada, b_ada, g_norm1, g_norm2, g_final, w_in, ret_norm_g, cmp_w1, cmp_pe, cmp_w2, w_ret_o, w_nsa_o,
           w_out, w_grp, b_grp, w_exp, b_exp, w_e1, w_e3, w_e2):
    depth = w_in.shape[0]
    assert depth == 1, "one layer"
    batch, seq, d = x_prompt.shape
    db, n_tok, _ = x_sample.shape
    n_pages = page_table.shape[1]
    past = n_pages * PAGE_SIZE
    win_buf = cache_win_kv.shape[2]
    assert win_buf <= seq
    tp = batch * seq
    ts = db * n_tok
    l = 0

    mod = _adaln(jnp.concatenate([c_prompt, c_sample], axis=0).astype(F32), w_ada[l], b_ada[l])
    mods = jnp.split(mod, 6, axis=-1)
    mod_p = [m[:batch].reshape(batch, 1, d) for m in mods]
    ts_tile = min(ts, 256)
    mod_s = [jnp.repeat(m[batch:], n_tok, axis=0).reshape(ts // ts_tile, ts_tile, d) for m in mods]

    w_pad = _pad_w_in(w_in[l])
    w_t = w_in[l][:, C_NQ:C_NG].T.astype(BF16)
    cmp_w = _cmp_weights(cmp_w1[l], cmp_pe[l], cmp_w2[l])
    wr, wn, wo = w_ret_o[l].astype(BF16), w_nsa_o[l].astype(BF16), w_out[l].astype(BF16)
    w_rt = jnp.concatenate([w_grp[l], w_exp[l], jnp.zeros((d, LANES - N_GROUPS - N_EXPERTS), F32)], axis=1)
    b_rt = jnp.concatenate([b_grp[l], b_exp[l], jnp.zeros((LANES - N_GROUPS - N_EXPERTS,), F32)]).reshape(1, LANES)

    tm_p = 256
    cos_p, sin_p = _rope_tables(jnp.arange(seq))
    (rq, rk, rv, rg, nqt, ckv_pt, skv_pt, wkv_pt, skb, wkb, ng, mg, svt, wvt) = _inproj(
        x_prompt.reshape(tp, d).astype(F32), mod_p[0], mod_p[1], g_norm1[l], cos_p, sin_p, w_pad, w_t,
        tm=tm_p, rows_per_mod=seq // tm_p, pos_blocks=seq // tm_p, act_dtype=BF16)
    yret_p, ret_state_p = _ret_prompt(rq, rk, rv, rg, ret_norm_g[l], batch=batch, seq=seq)
    ck, cvt = _cmp_prompt(ckv_pt, cmp_w, batch=batch, seq=seq)
    onsa_p = _nsa_prompt(nqt, ng, ck, cvt, skb, svt, wkb, wvt, batch=batch, seq=seq)
    tm_m = 512
    x1_p, h2_p, route_p = _mixout(x_prompt.reshape(tp, d).astype(F32), yret_p, onsa_p, mg, mod_p[2], mod_p[3],
                                  mod_p[4], g_norm2[l], wr, wn, wo, w_rt, b_rt, tm=tm_m, rows_per_mod=seq // tm_m)

    cos_s, sin_s = _rope_tables(past + jnp.arange(n_tok))
    cos_s = jnp.tile(cos_s, (ts_tile // n_tok, 1))
    sin_s = jnp.tile(sin_s, (ts_tile // n_tok, 1))
    (rq_s, rk_s, rv_s, rg_s, nq_s, ckv_s, skv_s, wkv_s, ng_s, mg_s) = _inproj(
        x_sample.reshape(ts, d).astype(F32), mod_s[0], mod_s[1], g_norm1[l], cos_s, sin_s, w_pad, None,
        tm=ts_tile, rows_per_mod=1, pos_blocks=1, act_dtype=F32)
    yret_s, ret_state_s = _ret_sample(rq_s, rk_s, rv_s, rg_s, state_ret[l].astype(F32), ret_norm_g[l], n_tok=n_tok)

    def feature_major(cache):
        n, r = cache.shape[:2]
        return jnp.transpose(cache, (0, 2, 3, 4, 1)).reshape(n, D_KV, r).astype(F32)

    onsa_s, win_s_t = _nsa_sample(nq_s, ng_s, skv_s, wkv_s, feature_major(cache_cmp_kv[l]),
                                  feature_major(cache_sel_kv[l]), feature_major(cache_win_kv[l]), page_table, cmp_w,
                                  n_tok=n_tok)
    def token_major(a_t):
        n, _, r = a_t.shape
        return jnp.transpose(a_t.reshape(n, 2, NSA_GROUPS, NSA_HEAD_DIM, r), (0, 4, 1, 2, 3))[None]

    x1_s, h2_s, route_s = _mixout(x_sample.reshape(ts, d).astype(F32), yret_s, onsa_s, mg_s, mod_s[2], mod_s[3],
                                  mod_s[4], g_norm2[l], wr, wn, wo, w_rt, b_rt, tm=ts_tile, rows_per_mod=1)

    bm = 256
    h2 = jnp.concatenate([h2_p, h2_s], axis=0)
    route = jnp.concatenate([route_p, route_s], axis=0)
    expert = route[:, :EXPERT_TOPK].astype(I32)
    dest, blk_expert, row_tok, n_used = _dispatch_plan(expert, bm=bm)
    y_buf = _moe_experts(h2, blk_expert, row_tok, n_used,
                         w_e1[l].astype(BF16), w_e3[l].astype(BF16), w_e2[l].astype(BF16), bm=bm)
    dest_flat = dest.reshape(-1)
    y_p = _final(dest_flat, x1_p, route_p, mod_p[5], g_final, y_buf, tm=tm_p, rows_per_mod=seq // tm_p, row0=0)
    y_s = _final(dest_flat, x1_s, route_s, mod_s[5], g_final, y_buf, tm=ts_tile, rows_per_mod=1, row0=tp)

    kv6 = lambda a, n, s: a.reshape(1, n, s, 2, NSA_GROUPS, NSA_HEAD_DIM)
    return (y_p.reshape(batch, seq, d).astype(x_prompt.dtype), y_s.reshape(db, n_tok, d).astype(x_sample.dtype),
            ret_state_p[None].astype(state_ret.dtype), ret_state_s[None].astype(state_ret.dtype),
            token_major(ckv_pt).astype(cache_cmp_kv.dtype), kv6(ckv_s, db, n_tok).astype(cache_cmp_kv.dtype),
            token_major(skv_pt).astype(cache_sel_kv.dtype), kv6(skv_s, db, n_tok).astype(cache_sel_kv.dtype),
            token_major(wkv_pt[:, :, seq - win_buf:]).astype(cache_win_kv.dtype),
            token_major(win_s_t).astype(cache_win_kv.dtype))
```

```python
import functools

import numpy as np
import jax
import jax.numpy as jnp
from jax import lax
from jax.experimental import pallas as pl
from jax.experimental.pallas import tpu as pltpu

F32 = jnp.float32
BF16 = jnp.bfloat16
I32 = jnp.int32

D_MODEL = 1024
PAGE_SIZE = 128
RET_HEADS = 8
RET_DK = 64
RET_DV = 128
ROPE_BASE = 10000.0
NSA_HEADS = 16
NSA_GROUPS = 4
NSA_HPG = NSA_HEADS // NSA_GROUPS
NSA_HEAD_DIM = 64
CMP_LEN = 32
CMP_STRIDE = 16
CMP_HID = 128
SEL_BLOCK = 64
SEL_TOPK = 16
WINDOW = 512
N_GROUPS = 4
EXPERTS_PER_GROUP = 8
N_EXPERTS = N_GROUPS * EXPERTS_PER_GROUP
EXPERT_TOPK = 2
D_EXPERT = 512
RMS_EPS = 1e-6
LOG2E = 1.4426950408889634
NEG_INF = -1e30
FORCE_SCORE = 1e9

LANES = 128
ONES_ROWS = 16
D_RET_QK = RET_HEADS * RET_DK
D_RET_V = RET_HEADS * RET_DV
D_NSA = NSA_HEADS * NSA_HEAD_DIM
D_KV = 2 * NSA_GROUPS * NSA_HEAD_DIM
D_GK = NSA_GROUPS * NSA_HEAD_DIM
N_GATE = 3 * NSA_HEADS
C_RQ = 0
C_RK = C_RQ + D_RET_QK
C_RV = C_RK + D_RET_QK
C_RG = C_RV + D_RET_V
C_NQ = C_RG + D_RET_V
C_CKV = C_NQ + D_NSA
C_SKV = C_CKV + D_KV
C_WKV = C_SKV + D_KV
C_NG = C_WKV + D_KV
C_MG = C_NG + LANES
C_END = C_MG + 2 * D_MODEL
VMEM_LIMIT = 56 << 20


def _dot(a, b):
    return jnp.dot(a, b, preferred_element_type=F32)


def _dot_nt(a, b):
    return lax.dot_general(a, b, (((1,), (1,)), ((), ())), preferred_element_type=F32)


def _dot_tn(a, b):
    return lax.dot_general(a, b, (((0,), (0,)), ((), ())), preferred_element_type=F32)


def _split(x):
    hi = x.astype(BF16)
    return hi, (x - hi.astype(F32)).astype(BF16)


def _dot3(a, w):
    ah, al = _split(a)
    wh, wl = _split(w)
    return _dot(ah, wh) + _dot(ah, wl) + _dot(al, wh)


def _silu(x):
    return x * jax.nn.sigmoid(x)


def _params(*sem):
    return pltpu.CompilerParams(dimension_semantics=sem, vmem_limit_bytes=VMEM_LIMIT)


def _adaln_kernel(c_ref, w_ref, b_ref, o_ref):
    o_ref[...] = _dot3(_silu(c_ref[...]), w_ref[...]) + b_ref[...]


def _adaln(c, w, b):
    r, d = c.shape
    n = w.shape[1]
    tn = 1024
    return pl.pallas_call(
        _adaln_kernel,
        grid=(n // tn,),
        in_specs=[pl.BlockSpec((r, d), lambda j: (0, 0)),
                  pl.BlockSpec((d, tn), lambda j: (0, j)),
                  pl.BlockSpec((1, tn), lambda j: (0, j))],
        out_specs=pl.BlockSpec((r, tn), lambda j: (0, j)),
        out_shape=jax.ShapeDtypeStruct((r, n), F32),
        compiler_params=_params("arbitrary"),
        name="adaln",
    )(c, w, b.reshape(1, n))


def _rope(x, cos, sin):
    outs = []
    lane = lax.broadcasted_iota(I32, cos.shape, 1)
    first = (lane & 32) == 0
    for j in range(x.shape[1] // LANES):
        xj = x[:, j * LANES:(j + 1) * LANES]
        partner = jnp.where(first, pltpu.roll(xj, LANES - 32, 1), pltpu.roll(xj, 32, 1))
        outs.append(xj * cos + partner * sin)
    return jnp.concatenate(outs, axis=1)


def _inproj_kernel(x_ref, sh_ref, sc_ref, g_ref, cos_ref, sin_ref, w_ref, *rest, key_major):
    if key_major:
        (wt_ref, rq_ref, rk_ref, rv_ref, rg_ref, nq_ref, ckv_ref, skv_ref, wkv_ref, skb_ref, wkb_ref, ng_ref,
         mg_ref, svt_ref, wvt_ref) = rest
    else:
        (rq_ref, rk_ref, rv_ref, rg_ref, nq_ref, ckv_ref, skv_ref, wkv_ref, ng_ref, mg_ref) = rest
    x = x_ref[...]
    h = x * lax.rsqrt(jnp.mean(x * x, axis=-1, keepdims=True) + RMS_EPS) * g_ref[...]
    hb = (h * (1.0 + sc_ref[...]) + sh_ref[...]).astype(BF16)

    def proj(a, b):
        return _dot(hb, w_ref[:, a:b])

    cos = cos_ref[...]
    sin = sin_ref[...]
    rq_ref[...] = _rope(proj(C_RQ, C_RK), cos, sin).astype(rq_ref.dtype)
    rk_ref[...] = (_rope(proj(C_RK, C_RV), cos, sin) * (RET_DK ** -0.5)).astype(rk_ref.dtype)
    rv_ref[...] = proj(C_RV, C_RG).astype(rv_ref.dtype)
    rg_ref[...] = proj(C_RG, C_NQ).astype(rg_ref.dtype)
    if key_major:
        n_t = nq_ref.shape[0]

        def proj_t(a, b):
            return _dot_nt(wt_ref[a:b, :], hb)

        def tiles(o_ref, y):
            for k in range(n_t):
                o_ref[k] = y[:, k * LANES:(k + 1) * LANES].astype(o_ref.dtype)

        tiles(nq_ref, proj_t(0, D_NSA) * (NSA_HEAD_DIM ** -0.5 * LOG2E))
        ckv_ref[...] = proj_t(D_NSA, D_NSA + D_KV)
        skv = proj_t(D_NSA + D_KV, D_NSA + 2 * D_KV)
        skv_ref[...] = skv
        tiles(svt_ref, skv[D_GK:])
        wkv = proj_t(D_NSA + 2 * D_KV, D_NSA + 3 * D_KV)
        wkv_ref[...] = wkv
        tiles(wvt_ref, wkv[D_GK:])
        skb_ref[...] = proj(C_SKV, C_SKV + D_GK).astype(BF16)
        wkb_ref[...] = proj(C_WKV, C_WKV + D_GK).astype(BF16)
    else:
        nq_ref[...] = (proj(C_NQ, C_CKV) * (NSA_HEAD_DIM ** -0.5)).astype(nq_ref.dtype)
        ckv_ref[...] = proj(C_CKV, C_SKV)
        skv_ref[...] = proj(C_SKV, C_WKV)
        wkv_ref[...] = proj(C_WKV, C_NG)
    ng_ref[...] = jax.nn.sigmoid(proj(C_NG, C_MG))
    mg_ref[...] = jax.nn.sigmoid(proj(C_MG, C_END)).astype(BF16)


def _inproj(x, shift, scale, g, cos, sin, w_pad, w_t, *, tm, rows_per_mod, pos_blocks, act_dtype):
    t, d = x.shape
    r = shift.shape[1]
    key_major = w_t is not None
    mod_spec = pl.BlockSpec((None, r, d), lambda i: (i // rows_per_mod, 0, 0))
    pos_spec = pl.BlockSpec((tm, LANES), lambda i: (i % pos_blocks, 0))

    def out(cols, dt):
        return pl.BlockSpec((tm, cols), lambda i: (i, 0)), jax.ShapeDtypeStruct((t, cols), dt)

    def out_t(rows):
        return (pl.BlockSpec((tm // LANES, rows, LANES), lambda i: (i, 0, 0)),
                jax.ShapeDtypeStruct((t // LANES, rows, LANES), BF16))

    def out_fm():
        return (pl.BlockSpec((None, D_KV, tm), lambda i: (i // rows_per_mod, 0, i % rows_per_mod)),
                jax.ShapeDtypeStruct((t // (tm * rows_per_mod), D_KV, tm * rows_per_mod), F32))

    outs = [out(D_RET_QK, act_dtype), out(D_RET_QK, act_dtype), out(D_RET_V, act_dtype), out(D_RET_V, act_dtype)]
    in_specs = [pl.BlockSpec((tm, d), lambda i: (i, 0)), mod_spec, mod_spec,
                pl.BlockSpec((1, d), lambda i: (0, 0)), pos_spec, pos_spec,
                pl.BlockSpec((d, C_END), lambda i: (0, 0), pipeline_mode=pl.Buffered(1))]
    args = [x, shift, scale, g.reshape(1, d), cos, sin, w_pad]
    if key_major:
        outs += [out_t(D_NSA), out_fm(), out_fm(), out_fm(), out(D_GK, BF16), out(D_GK, BF16),
                 out(LANES, F32), out(2 * D_MODEL, BF16), out_t(D_GK), out_t(D_GK)]
        in_specs.append(pl.BlockSpec(w_t.shape, lambda i: (0, 0), pipeline_mode=pl.Buffered(1)))
        args.append(w_t)
    else:
        outs += [out(D_NSA, act_dtype), out(D_KV, F32), out(D_KV, F32), out(D_KV, F32),
                 out(LANES, F32), out(2 * D_MODEL, BF16)]
    return pl.pallas_call(
        functools.partial(_inproj_kernel, key_major=key_major),
        grid=(t // tm,),
        in_specs=in_specs,
        out_specs=[o[0] for o in outs],
        out_shape=[o[1] for o in outs],
        compiler_params=_params("arbitrary"),
        name="inproj",
    )(*args)


def _rope_tables(pos):
    half = RET_DK // 2
    inv_freq = ROPE_BASE ** (-jnp.arange(half, dtype=F32) / half)
    ang = pos.astype(F32)[:, None] * inv_freq[None, :]
    cos, sin = jnp.cos(ang), jnp.sin(ang)
    return jnp.tile(cos, (1, 4)), jnp.tile(jnp.concatenate([-sin, sin], axis=1), (1, 2))


def _pad_w_in(w_in):
    d = w_in.shape[0]
    n_real = C_NG + N_GATE
    return jnp.concatenate([w_in[:, :n_real], jnp.zeros((d, LANES - N_GATE), w_in.dtype), w_in[:, n_real:]],
                           axis=1).astype(BF16)


def _ret_log_decay():
    return np.log1p(-np.exp2(-5.0 - np.arange(RET_HEADS, dtype=np.float32))).astype(np.float32)


def _ret_head_out(o, g, gn):
    n = o * lax.rsqrt(jnp.mean(o * o, axis=-1, keepdims=True) + RMS_EPS) * gn
    return _silu(g) * n


def _ret_prompt_kernel(q_ref, k_ref, v_ref, g_ref, dm_ref, qd_ref, kd_ref, gn_ref, y_ref, sf_ref, s_scr, *, sdec):
    j = pl.program_id(1)

    @pl.when(j == 0)
    def _():
        s_scr[...] = jnp.zeros_like(s_scr)

    qb = q_ref[...]
    kb = k_ref[...]
    qd = (qb.astype(F32) * qd_ref[...]).astype(BF16)
    kd = (kb.astype(F32) * kd_ref[...]).astype(BF16)
    gn = gn_ref[...]
    heads = range(RET_HEADS)
    sl = [slice(h * RET_DK, (h + 1) * RET_DK) for h in heads]
    vl = [slice(h * RET_DV, (h + 1) * RET_DV) for h in heads]
    vs = [v_ref[:, vl[h]] for h in heads]
    s_old = [s_scr[h] for h in heads]
    scs = [(_dot_nt(qb[:, sl[h]], kb[:, sl[h]]) * dm_ref[h]).astype(BF16) for h in heads]
    outs = [_dot(scs[h], vs[h]) + _dot(qd[:, sl[h]], s_old[h].astype(BF16)) for h in heads]
    for h in heads:
        s_scr[h] = sdec[h] * s_old[h] + _dot_tn(kd[:, sl[h]], vs[h])
    for h in heads:
        y_ref[:, vl[h]] = _ret_head_out(outs[h], g_ref[:, vl[h]].astype(F32), gn).astype(y_ref.dtype)

    @pl.when(j == pl.num_programs(1) - 1)
    def _():
        sf_ref[...] = s_scr[...]


def _ret_tables(c):
    lg = _ret_log_decay()
    i = np.arange(c, dtype=np.float32)
    diff = i[:, None] - i[None, :]
    dm = np.where(diff[None] >= 0, np.exp(lg[:, None, None] * np.maximum(diff, 0.0)[None]), 0.0).astype(np.float32)
    qd = np.repeat(np.exp(lg[None, :] * (i[:, None] + 1.0)), RET_DK, axis=1).astype(np.float32)
    kd = np.repeat(np.exp(lg[None, :] * (c - 1.0 - i)[:, None]), RET_DK, axis=1).astype(np.float32)
    sdec = tuple(float(v) for v in np.exp(lg * np.float32(c)))
    return dm, qd, kd, sdec


def _ret_prompt(rq, rk, rv, rg, gn, *, batch, seq):
    c = 128
    n = seq // c
    dm, qd, kd, sdec = _ret_tables(c)
    row = lambda b, j: (b * n + j, 0)
    const2 = lambda b, j: (0, 0)
    return pl.pallas_call(
        functools.partial(_ret_prompt_kernel, sdec=sdec),
        grid=(batch, n),
        in_specs=[pl.BlockSpec((c, D_RET_QK), row), pl.BlockSpec((c, D_RET_QK), row),
                  pl.BlockSpec((c, D_RET_V), row), pl.BlockSpec((c, D_RET_V), row),
                  pl.BlockSpec((RET_HEADS, c, c), lambda b, j: (0, 0, 0)),
                  pl.BlockSpec((c, D_RET_QK), const2), pl.BlockSpec((c, D_RET_QK), const2),
                  pl.BlockSpec((1, RET_DV), const2)],
        out_specs=[pl.BlockSpec((c, D_RET_V), row),
                   pl.BlockSpec((None, RET_HEADS, RET_DK, RET_DV), lambda b, j: (b, 0, 0, 0))],
        out_shape=[jax.ShapeDtypeStruct((batch * seq, D_RET_V), BF16),
                   jax.ShapeDtypeStruct((batch, RET_HEADS, RET_DK, RET_DV), F32)],
        scratch_shapes=[pltpu.VMEM((RET_HEADS, RET_DK, RET_DV), F32)],
        compiler_params=_params("arbitrary", "arbitrary"),
        name="ret_prompt",
    )(rq, rk, rv, rg, jnp.asarray(dm), jnp.asarray(qd), jnp.asarray(kd), gn.reshape(1, RET_DV))


def _ret_sample_kernel(q_ref, k_ref, v_ref, g_ref, s0_ref, dm_ref, qd_ref, kd_ref, gn_ref, y_ref, sn_ref,
                       *, sdec, n_req, n_tok):
    rows = n_req * n_tok
    q = q_ref[...]
    k = k_ref[...]
    qd = q * qd_ref[...]
    kd = k * kd_ref[...]
    qb = q.astype(BF16)
    kb = k.astype(BF16)
    gn = gn_ref[...]
    req = lax.broadcasted_iota(I32, (rows, 1), 0) // n_tok
    for h in range(RET_HEADS):
        sl = slice(h * RET_DK, (h + 1) * RET_DK)
        vl = slice(h * RET_DV, (h + 1) * RET_DV)
        vh = v_ref[:, vl].astype(BF16)
        sc = _dot_nt(qb[:, sl], kb[:, sl]) * dm_ref[h]
        o = _dot(sc.astype(BF16), vh)
        qd_h = qd[:, sl]
        kd_h = kd[:, sl]
        for r in range(n_req):
            own = req == r
            s_r = s0_ref[r, h]
            o = o + _dot(jnp.where(own, qd_h, 0.0).astype(BF16), s_r.astype(BF16))
            sn_ref[r, h] = sdec[h] * s_r + _dot_tn(jnp.where(own, kd_h, 0.0).astype(BF16), vh)
        y_ref[:, vl] = _ret_head_out(o, g_ref[:, vl], gn).astype(y_ref.dtype)


def _ret_sample(rq, rk, rv, rg, state, gn, *, n_tok):
    db = state.shape[0]
    n_req = 16
    rows = n_req * n_tok
    lg = _ret_log_decay()
    i = np.arange(rows)
    t = (i % n_tok).astype(np.float32)
    same = (i[:, None] // n_tok) == (i[None, :] // n_tok)
    diff = t[:, None] - t[None, :]
    dm = np.where((same & (diff >= 0))[None], np.exp(lg[:, None, None] * np.maximum(diff, 0.0)[None]), 0.0)
    qd = np.repeat(np.exp(lg[None, :] * (t[:, None] + 1.0)), RET_DK, axis=1)
    kd = np.repeat(np.exp(lg[None, :] * (n_tok - 1.0 - t)[:, None]), RET_DK, axis=1)
    sdec = tuple(float(v) for v in np.exp(lg * np.float32(n_tok)))
    row = lambda i: (i, 0)
    const2 = lambda i: (0, 0)
    st_spec = pl.BlockSpec((n_req, RET_HEADS, RET_DK, RET_DV), lambda i: (i, 0, 0, 0))
    return pl.pallas_call(
        functools.partial(_ret_sample_kernel, sdec=sdec, n_req=n_req, n_tok=n_tok),
        grid=(db // n_req,),
        in_specs=[pl.BlockSpec((rows, D_RET_QK), row), pl.BlockSpec((rows, D_RET_QK), row),
                  pl.BlockSpec((rows, D_RET_V), row), pl.BlockSpec((rows, D_RET_V), row), st_spec,
                  pl.BlockSpec((RET_HEADS, rows, rows), lambda i: (0, 0, 0)),
                  pl.BlockSpec((rows, D_RET_QK), const2), pl.BlockSpec((rows, D_RET_QK), const2),
                  pl.BlockSpec((1, RET_DV), const2)],
        out_specs=[pl.BlockSpec((rows, D_RET_V), row), st_spec],
        out_shape=[jax.ShapeDtypeStruct((db * n_tok, D_RET_V), BF16),
                   jax.ShapeDtypeStruct(state.shape, F32)],
        compiler_params=_params("arbitrary"),
        name="ret_sample",
    )(rq, rk, rv, rg, state, jnp.asarray(dm, F32), jnp.asarray(qd, F32), jnp.asarray(kd, F32),
      gn.reshape(1, RET_DV))


def _cmp_weights(cmp_w1, cmp_pe, cmp_w2):
    w1c = jnp.concatenate([cmp_w1[:, :CMP_STRIDE], cmp_w1[:, CMP_STRIDE:]], axis=-1).astype(BF16)
    w1c = w1c.reshape(2, CMP_STRIDE // 4, 4 * NSA_HEAD_DIM, 2 * CMP_HID)
    eye = jnp.eye(NSA_GROUPS, dtype=cmp_w2.dtype)
    w2bd = jnp.einsum('gk,chd->cghkd', eye, cmp_w2).reshape(2, NSA_GROUPS * CMP_HID, D_GK).astype(BF16)
    pe8 = jnp.broadcast_to(cmp_pe.reshape(2, 1, CMP_LEN * NSA_HEAD_DIM), (2, 8, CMP_LEN * NSA_HEAD_DIM))
    w1f = cmp_w1.reshape(2, CMP_LEN * NSA_HEAD_DIM, CMP_HID)
    r = np.arange(PAGE_SIZE)
    seg_pp = PAGE_SIZE // CMP_STRIDE
    perm = r[None, :] == (r[:, None] % seg_pp) * CMP_STRIDE + r[:, None] // seg_pp
    return jnp.asarray(perm, BF16), w1c, w2bd, jnp.swapaxes(w2bd, 1, 2), pe8, w1f


def _cmp_pe_term(pe8_ref, w1f_ref, c):
    return _dot3(pe8_ref[c], w1f_ref[c])[0:1]


def _cmp_partial(pages, perm, w1c_ref):
    seg_pp = PAGE_SIZE // CMP_STRIDE
    kv = range(2)
    ys = [[_dot_nt(perm, pg[c * D_GK:(c + 1) * D_GK, :].astype(BF16)) for pg in pages] for c in kv]
    ysw = [[jnp.concatenate([pltpu.roll(y[:, :LANES], NSA_HEAD_DIM, 1), pltpu.roll(y[:, LANES:], NSA_HEAD_DIM, 1)],
                            axis=1) for y in ys[c]] for c in kv]
    lo_half = lax.broadcasted_iota(I32, (seg_pp * len(pages), LANES), 1) < NSA_HEAD_DIM

    def piece(src, p, col):
        return jnp.concatenate([y[p * seg_pp:(p + 1) * seg_pp, col * LANES:(col + 1) * LANES] for y in src], axis=0)

    def stacked(c, q):
        rows = []
        for g in range(NSA_GROUPS):
            src_lo = ys[c] if g % 2 == 0 else ysw[c]
            src_hi = ysw[c] if g % 2 == 0 else ys[c]
            cols = [jnp.where(lo_half, piece(src_lo, 4 * q + 2 * j, g // 2), piece(src_hi, 4 * q + 2 * j + 1, g // 2))
                    for j in range(2)]
            rows.append(jnp.concatenate(cols, axis=1))
        return jnp.concatenate(rows, axis=0).astype(BF16)

    n_q = CMP_STRIDE // 4
    lhs = [[stacked(c, q) for q in range(n_q)] for c in kv]
    accs = []
    for c in kv:
        acc = _dot(lhs[c][0], w1c_ref[c, 0])
        for q in range(1, n_q):
            acc = acc + _dot(lhs[c][q], w1c_ref[c, q])
        accs.append(acc)
    return accs


def _cmp_store(pages, perm, w1c_ref, lohi_scr, step):
    seg_ps = len(pages) * (PAGE_SIZE // CMP_STRIDE)
    off = pl.multiple_of(step * seg_ps, seg_ps)
    for c, acc in enumerate(_cmp_partial(pages, perm, w1c_ref)):
        for g in range(NSA_GROUPS):
            lohi_scr[c, g, pl.ds(off, seg_ps), :] = acc[g * seg_ps:(g + 1) * seg_ps]


def _cmp_finish(lohi, pe, w2_c, feature_major):
    hs = []
    for g in range(NSA_GROUPS):
        lh = lohi(g)
        nseg = lh.shape[0]
        z = lh[:, :CMP_HID] + pltpu.roll(lh[:, CMP_HID:], nseg - 1, 0) + pe
        hs.append(_silu(z).astype(BF16))
    hid = jnp.concatenate(hs, axis=1)
    if feature_major:
        return _dot_nt(w2_c, hid)
    return _dot(hid, w2_c)


def _q_blockdiag(qf, tq):
    lane = lax.broadcasted_iota(I32, (tq, D_GK), 1)
    blocks = []
    for g in range(NSA_GROUPS):
        slab = qf[:, g * D_GK:(g + 1) * D_GK]
        keep = (lane >= g * NSA_HEAD_DIM) & (lane < (g + 1) * NSA_HEAD_DIM)
        for r in range(NSA_HPG):
            sh = (NSA_HEAD_DIM * (g - r)) % D_GK
            rolled = slab if sh == 0 else pltpu.roll(slab, sh, 1)
            blocks.append(jnp.where(keep, rolled, 0.0))
    return jnp.concatenate(blocks, axis=0).astype(BF16)


def _pair_pv(p, v):
    half = p.shape[0] // 2
    return jnp.concatenate([_dot(p[:half], v[:, :LANES]), _dot(p[half:], v[:, LANES:])], axis=0)


def _assemble_heads(acc, tq):
    lane = lax.broadcasted_iota(I32, (tq, LANES), 1)
    chunks = []
    for c in range(NSA_HEADS // 2):
        a = acc[(2 * c) * tq:(2 * c + 1) * tq]
        b = acc[(2 * c + 1) * tq:(2 * c + 2) * tq]
        if (c // 2) % 2 == 0:
            b = pltpu.roll(b, NSA_HEAD_DIM, 1)
        else:
            a = pltpu.roll(a, NSA_HEAD_DIM, 1)
        chunks.append(jnp.where(lane < NSA_HEAD_DIM, a, b))
    return jnp.concatenate(chunks, axis=1)


def _gate_branches(ng, eg_ref, branches, tq):
    nh, nl = _split(ng)
    out = None
    for br, acc in enumerate(branches):
        gexp = _dot(nh, eg_ref[br]) + _dot(nl, eg_ref[br])
        term = gexp * _assemble_heads(acc, tq)
        out = term if out is None else out + term
    return out


def _masked_softmax(s, mask):
    s = jnp.where(mask, s, NEG_INF)
    m = jnp.max(s, axis=-1, keepdims=True)
    e = jnp.exp(s - m)
    return e / jnp.sum(e, axis=-1, keepdims=True)


def _rank_select(sc, sc_scr, valid, n_rows):
    sc_scr[...] = sc
    n_io = lax.broadcasted_iota(I32, sc.shape, 0)

    def body(m, cnt):
        row = sc_scr[pl.ds(m, 1), :]
        beats = (row > sc) | ((row == sc) & (m < n_io))
        return cnt + jnp.where(beats, 1, 0)

    cnt = lax.fori_loop(0, n_rows, body, jnp.zeros(sc.shape, I32))
    return jnp.where((cnt < SEL_TOPK) & valid, 1.0, 0.0)


def _topk_select(sc, valid, tri_ref):
    nb = sc.shape[0]
    bits = pltpu.bitcast(sc, I32)
    key = bits ^ ((bits >> 31) & jnp.int32(0x7FFFFFFF))

    def body(i, t):
        cand = t + (jnp.int32(1) << (31 - i))
        cnt = jnp.sum(jnp.where(key >= cand, 1.0, 0.0), axis=0, keepdims=True)
        return jnp.where(cnt >= SEL_TOPK, cand, t)

    t = lax.fori_loop(0, 32, body, jnp.full((1, sc.shape[1]), -2 ** 31, I32))
    gt = key > t
    eq = key == t
    need = SEL_TOPK - jnp.sum(jnp.where(gt, 1.0, 0.0), axis=0, keepdims=True)
    eq_pad = jnp.concatenate([jnp.where(eq, 1.0, 0.0), jnp.zeros((tri_ref.shape[1] - nb, sc.shape[1]), F32)], axis=0)
    before = _dot(tri_ref[...], eq_pad.astype(BF16))[:nb]
    return jnp.where((gt | (eq & (before < need))) & valid, 1.0, 0.0)


def _block_scores(scT, pos, n_io):
    valid = n_io * SEL_BLOCK <= pos
    cur = pos // SEL_BLOCK
    forced = (n_io == 0) | (n_io == cur) | (n_io == cur - 1)
    return jnp.where(forced, FORCE_SCORE, jnp.where(valid, scT, -FORCE_SCORE)), valid


def _nsa_tables(n_cmp_pad, n_cmp, n_blocks, nb_pad, n_keys_pad, tk, key_major=False):
    c = np.arange(n_cmp_pad)[None, :]
    n = np.arange(nb_pad)[:, None]
    ov = np.clip(np.minimum(CMP_STRIDE * c + CMP_LEN, SEL_BLOCK * n + SEL_BLOCK)
                 - np.maximum(CMP_STRIDE * c, SEL_BLOCK * n), 0, None) / CMP_STRIDE
    ov = np.where((c < n_cmp) & (n < n_blocks), ov, 0.0)
    key = np.arange(n_keys_pad)
    e = (key[None, :] // SEL_BLOCK == np.arange(LANES * ((nb_pad + LANES - 1) // LANES))[:, None])
    e3 = e.reshape(e.shape[0], n_keys_pad // tk, tk).transpose(1, 0, 2)
    if key_major:
        e3 = e3.transpose(0, 2, 1)
    col = np.arange(LANES)[:, None]
    lane = np.arange(D_NSA)[None, :]
    eg = np.stack([col == br * NSA_HEADS + lane // NSA_HEAD_DIM for br in range(3)])
    return jnp.asarray(ov, BF16), jnp.asarray(e3, BF16), jnp.asarray(eg, BF16)


def _cmp_prompt_kernel(ck_ref, perm_ref, w1c_ref, w2_ref, w2t_ref, pe8_ref, w1f_ref, ko_ref, vto_ref, lohi_scr):
    st = pl.program_id(1)
    n_pg = ck_ref.shape[1] // PAGE_SIZE
    pages = [ck_ref[:, k * PAGE_SIZE:(k + 1) * PAGE_SIZE] for k in range(n_pg)]
    _cmp_store(pages, perm_ref[...], w1c_ref, lohi_scr, st)

    @pl.when(st == pl.num_programs(1) - 1)
    def _():
        ko_ref[...] = _cmp_finish(lambda g: lohi_scr[0, g], _cmp_pe_term(pe8_ref, w1f_ref, 0), w2_ref[0],
                                  False).astype(ko_ref.dtype)
        vto_ref[...] = _cmp_finish(lambda g: lohi_scr[1, g], _cmp_pe_term(pe8_ref, w1f_ref, 1), w2t_ref[1],
                                   True).astype(vto_ref.dtype)


def _cmp_prompt(ckv_t, cmp_w, *, batch, seq):
    nseg = seq // CMP_STRIDE
    rows = 8 * PAGE_SIZE
    n_st = seq // rows
    full = lambda a: pl.BlockSpec(a.shape, lambda b, s: (0,) * a.ndim)
    return pl.pallas_call(
        _cmp_prompt_kernel,
        grid=(batch, n_st),
        in_specs=[pl.BlockSpec((None, D_KV, rows), lambda b, s: (b, 0, s))] + [full(a) for a in cmp_w],
        out_specs=[pl.BlockSpec((nseg, D_GK), lambda b, s: (b, 0)),
                   pl.BlockSpec((None, D_GK, nseg), lambda b, s: (b, 0, 0))],
        out_shape=[jax.ShapeDtypeStruct((batch * nseg, D_GK), BF16),
                   jax.ShapeDtypeStruct((batch, D_GK, nseg), BF16)],
        scratch_shapes=[pltpu.VMEM((2, NSA_GROUPS, nseg, 2 * CMP_HID), F32)],
        compiler_params=_params("arbitrary", "arbitrary"),
        name="cmp_prompt",
    )(ckv_t, *cmp_w)


def _col_softmax(s, mask):
    s = jnp.where(mask, s, NEG_INF)
    return jnp.exp2(s - jnp.max(s, axis=0, keepdims=True))


def _nsa_prompt_kernel(qt_ref, ng_ref, ck_ref, cvt_ref, sk_ref, svt_ref, wk_ref, wvt_ref, ov_ref, et_ref,
                       o_ref, sc_scr, *gscr, tq, tk, wlen):
    seq = sk_ref.shape[0]
    nseg = ck_ref.shape[0]
    nb = ov_ref.shape[0]
    gl = NSA_HPG * tq
    hd = NSA_HEAD_DIM
    q0 = pl.program_id(1) * tq

    qt = qt_ref[0]
    qg = []
    for g in range(NSA_GROUPS):
        cols = []
        for r in range(NSA_HPG):
            h = NSA_HPG * g + r
            parts = [qt[h * hd:(h + 1) * hd, :]]
            if g > 0:
                parts.insert(0, jnp.zeros((hd * g, tq), BF16))
            if g < NSA_GROUPS - 1:
                parts.append(jnp.zeros((hd * (NSA_GROUPS - 1 - g), tq), BF16))
            cols.append(jnp.concatenate(parts, axis=0))
        qg.append(jnp.concatenate(cols, axis=1))
    pos_t = q0 + lax.broadcasted_iota(I32, (1, gl), 1) % tq

    ck = ck_ref[...]
    cidx = lax.broadcasted_iota(I32, (nseg, gl), 0)
    cmask = (CMP_STRIDE * cidx + (CMP_LEN - 1) <= pos_t) & (cidx < nseg - 1)
    o_cmp, pgs = [], []
    ss = [_dot(ck, qg[g]) for g in range(NSA_GROUPS)]
    for g in range(NSA_GROUPS):
        e = _col_softmax(ss[g], cmask)
        p = jnp.where(cmask, e * (1.0 / jnp.sum(e, axis=0, keepdims=True)), 0.0)
        pgs.append(p[:, 0:tq] + p[:, tq:2 * tq] + p[:, 2 * tq:3 * tq] + p[:, 3 * tq:4 * tq])
        ss[g] = p.astype(BF16)
    for g in range(NSA_GROUPS):
        o_cmp.append(_dot(cvt_ref[g * hd:(g + 1) * hd, :], ss[g]))
    ph, plo = _split(jnp.concatenate(pgs, axis=1))
    sc_t = _dot(ov_ref[...], ph) + _dot(ov_ref[...], plo)

    n_io = lax.broadcasted_iota(I32, sc_t.shape, 0)
    pos_l = q0 + lax.broadcasted_iota(I32, sc_t.shape, 1) % tq
    sc, valid = _block_scores(sc_t, pos_l, n_io)
    sel_t = _rank_select(sc, sc_scr, valid, jnp.minimum((q0 + tq - 1) // SEL_BLOCK + 1, nb))
    selb = jnp.concatenate([sel_t, jnp.zeros((LANES - nb, sc_t.shape[1]), F32)], axis=0).astype(BF16)

    m_scr, acc_scr = gscr[:NSA_GROUPS], gscr[NSA_GROUPS:]
    for g in range(NSA_GROUPS):
        m_scr[g][...] = jnp.full(m_scr[g].shape, NEG_INF, F32)
        acc_scr[g][...] = jnp.zeros(acc_scr[g].shape, F32)
    key_i = lax.broadcasted_iota(I32, (tk, tq), 0)
    pos_q = q0 + lax.broadcasted_iota(I32, (tk, tq), 1)
    ones_k = jnp.ones((ONES_ROWS, tk), BF16)

    def kv_step(j, carry, on_diagonal):
        k0 = pl.multiple_of(j * tk, tk)
        kt = sk_ref[pl.ds(k0, tk), :]
        selm = _dot(et_ref[j], selb)
        ss, ms = [], []
        for g in range(NSA_GROUPS):
            ok = selm[:, g * tq:(g + 1) * tq] > 0.5
            if on_diagonal:
                ok = ok & (k0 + key_i <= pos_q)
            bias = jnp.where(ok, 0.0, NEG_INF)
            s = _dot(kt, qg[g]) + jnp.concatenate([bias] * NSA_HPG, axis=1)
            ss.append(s)
            ms.append(jnp.maximum(m_scr[g][...], jnp.max(s, axis=0, keepdims=True)))
        ps = [jnp.exp2(ss[g] - ms[g]).astype(BF16) for g in range(NSA_GROUPS)]
        for g in range(NSA_GROUPS):
            vt = jnp.concatenate([svt_ref[j * (tk // LANES) + u, g * hd:(g + 1) * hd, :]
                                  for u in range(tk // LANES)], axis=1)
            alpha = jnp.exp2(m_scr[g][...] - ms[g])
            acc_scr[g][...] = alpha * acc_scr[g][...] + _dot(jnp.concatenate([vt, ones_k], axis=0), ps[g])
            m_scr[g][...] = ms[g]
        return carry

    n_before = q0 // tk
    lax.fori_loop(0, n_before, functools.partial(kv_step, on_diagonal=False), 0)
    lax.fori_loop(n_before, (q0 + tq + tk - 1) // tk, functools.partial(kv_step, on_diagonal=True), 0)

    w0 = pl.multiple_of(jnp.clip(q0 + tq - wlen, 0, seq - wlen), LANES)
    kw = wk_ref[pl.ds(w0, wlen), :]
    dist = pos_t - (w0 + lax.broadcasted_iota(I32, (wlen, gl), 0))
    wmask = (dist >= 0) & (dist < WINDOW)
    o_win = []
    ones_w = jnp.ones((ONES_ROWS, wlen), BF16)
    ss = [_dot(kw, qg[g]) for g in range(NSA_GROUPS)]
    ss = [_col_softmax(ss[g], wmask).astype(BF16) for g in range(NSA_GROUPS)]
    for g in range(NSA_GROUPS):
        vt = jnp.concatenate([wvt_ref[w0 // LANES + u, g * hd:(g + 1) * hd, :] for u in range(wlen // LANES)], axis=1)
        ow = _dot(jnp.concatenate([vt, ones_w], axis=0), ss[g])
        o_win.append(ow[:hd] * (1.0 / ow[hd:hd + 1]))

    ngt = ng_ref[...].T
    heads = []
    for g in range(NSA_GROUPS):
        acc = acc_scr[g][...]
        o_sel = acc[:hd] * (1.0 / acc[hd:hd + 1])
        for r in range(NSA_HPG):
            h = NSA_HPG * g + r
            lanes = slice(r * tq, (r + 1) * tq)
            heads.append(ngt[h:h + 1, :] * o_cmp[g][:, lanes]
                         + ngt[NSA_HEADS + h:NSA_HEADS + h + 1, :] * o_sel[:, lanes]
                         + ngt[2 * NSA_HEADS + h:2 * NSA_HEADS + h + 1, :] * o_win[g][:, lanes])
    ot = jnp.concatenate(heads, axis=0)
    o_ref[...] = jnp.concatenate([ot[c * LANES:(c + 1) * LANES].T for c in range(D_NSA // LANES)],
                                 axis=1).astype(o_ref.dtype)


def _nsa_prompt(nqt, ng, ck, cvt, skb, svt, wkb, wvt, *, batch, seq):
    tq, tk, wlen = LANES, 512, 640
    assert seq % tk == 0 and seq >= wlen and wlen >= WINDOW + tq
    nseg = seq // CMP_STRIDE
    n_blocks = seq // SEL_BLOCK
    nq_t = seq // tq
    ov, et, _ = _nsa_tables(nseg, nseg - 1, n_blocks, n_blocks, seq, tk, key_major=True)
    gl = NSA_HPG * tq
    qrow = lambda b, i: (b * nq_t + i, 0)
    full = lambda a: pl.BlockSpec(a.shape, lambda b, i: (0,) * a.ndim)
    kspec = pl.BlockSpec((seq, D_GK), lambda b, i: (b, 0))
    vtspec = pl.BlockSpec((seq // LANES, D_GK, LANES), lambda b, i: (b, 0, 0))
    return pl.pallas_call(
        functools.partial(_nsa_prompt_kernel, tq=tq, tk=tk, wlen=wlen),
        grid=(batch, nq_t),
        in_specs=[pl.BlockSpec((1, D_NSA, tq), lambda b, i: (b * nq_t + i, 0, 0)), pl.BlockSpec((tq, LANES), qrow),
                  pl.BlockSpec((nseg, D_GK), lambda b, i: (b, 0)),
                  pl.BlockSpec((None, D_GK, nseg), lambda b, i: (b, 0, 0)),
                  kspec, vtspec, kspec, vtspec, full(ov), full(et)],
        out_specs=pl.BlockSpec((tq, D_NSA), qrow),
        out_shape=jax.ShapeDtypeStruct((batch * seq, D_NSA), BF16),
        scratch_shapes=[pltpu.VMEM((n_blocks, NSA_GROUPS * tq), F32)]
        + [pltpu.VMEM((1, gl), F32)] * NSA_GROUPS
        + [pltpu.VMEM((NSA_HEAD_DIM + ONES_ROWS, gl), F32)] * NSA_GROUPS,
        compiler_params=_params("arbitrary", "arbitrary"),
        name="nsa_prompt",
    )(nqt, ng, ck, cvt, skb, svt, wkb, wvt, ov, et)


def _nsa_sample_kernel(pt_ref, *refs, n_tok, ppb, n_blocks, past):
    cmp_refs = refs[:ppb]
    sel_refs = refs[ppb:2 * ppb]
    (q_ref, ng_ref, snew_ref, wnew_ref, wint_ref, perm_ref, w1c_ref, w2_ref, w2t_ref, pe8_ref, w1f_ref, ov_ref,
     e_ref, eg_ref, tri_ref, o_ref, wout_ref, lohi_scr, sel_scr, s_scr, pe_scr) = refs[2 * ppb:]
    del pt_ref
    b = pl.program_id(0)
    st = pl.program_id(1)
    n_st = sel_scr.shape[0]
    nseg = lohi_scr.shape[2]
    tq = n_tok
    rows = NSA_HEADS * tq
    tkp = ppb * PAGE_SIZE

    @pl.when((b == 0) & (st == 0))
    def _():
        for c in range(2):
            pe_scr[c] = jnp.broadcast_to(_cmp_pe_term(pe8_ref, w1f_ref, c), (8, CMP_HID))

    _cmp_store([r[...] for r in cmp_refs], perm_ref[...], w1c_ref, lohi_scr, st)
    for k, r in enumerate(sel_refs):
        sel_scr[st, :, k * PAGE_SIZE:(k + 1) * PAGE_SIZE] = r[...].astype(BF16)

    @pl.when(st == n_st - 1)
    def _():
        qbd = _q_blockdiag(q_ref[...], tq)
        pos_r = past + lax.broadcasted_iota(I32, (rows, 1), 0) % tq
        pad = jnp.zeros((LANES - tq, D_KV), F32)
        snew_t = jnp.concatenate([snew_ref[...], pad], axis=0).T.astype(BF16)
        wnew_t = jnp.concatenate([wnew_ref[...], pad], axis=0).T

        kct = _cmp_finish(lambda g: lohi_scr[0, g], pe_scr[0, 0:1], w2t_ref[0], True).astype(BF16)
        vc = _cmp_finish(lambda g: lohi_scr[1, g], pe_scr[1, 0:1], w2_ref[1], False).astype(BF16)
        s = _dot(qbd, kct)
        cidx = lax.broadcasted_iota(I32, (rows, nseg), 1)
        cmask = (CMP_STRIDE * cidx + (CMP_LEN - 1) <= pos_r) & (cidx < nseg - 1)
        p = jnp.where(cmask, _masked_softmax(s, cmask), 0.0)
        o_cmp = _pair_pv(p.astype(BF16), vc)
        pg = jnp.sum(p.reshape(NSA_GROUPS, NSA_HPG, tq, nseg), axis=1, keepdims=True)
        pg = jnp.broadcast_to(pg, (NSA_GROUPS, NSA_HPG, tq, nseg)).reshape(rows, nseg)
        ph, plo = _split(pg)
        sc_t = _dot_nt(ov_ref[...], ph) + _dot_nt(ov_ref[...], plo)

        nb_pad = ov_ref.shape[0]
        nb8 = -(-n_blocks // 8) * 8
        n_io = lax.broadcasted_iota(I32, (nb8, rows), 0)
        pos_l = past + lax.broadcasted_iota(I32, (nb8, rows), 1) % tq
        sc, valid = _block_scores(sc_t[:nb8], pos_l, n_io)
        sel_t = _topk_select(sc, valid & (n_io < n_blocks), tri_ref)
        sel = jnp.concatenate([sel_t, jnp.zeros((nb_pad - nb8, rows), F32)], axis=0).T.astype(BF16)

        for t in range(n_st):
            selm = _dot(sel, e_ref[t])
            s_scr[:, t * tkp:(t + 1) * tkp] = jnp.where(selm > 0.5, _dot(qbd, sel_scr[t, 0:D_GK, :]), NEG_INF)
        selm = _dot(sel, e_ref[n_st, :, 0:LANES])
        new_ok = (selm > 0.5) & (past + lax.broadcasted_iota(I32, (rows, LANES), 1) <= pos_r)
        s_scr[:, n_st * tkp:] = jnp.where(new_ok, _dot(qbd, snew_t[0:D_GK]), NEG_INF)
        s_all = s_scr[...]
        ps = jnp.exp(s_all - jnp.max(s_all, axis=-1, keepdims=True))
        l_sel = jnp.sum(ps, axis=-1, keepdims=True)
        ps = ps.astype(BF16)
        acc = _dot_nt(ps[:, n_st * tkp:], snew_t[D_GK:])
        for t in range(n_st):
            acc = acc + _dot_nt(ps[:, t * tkp:(t + 1) * tkp], sel_scr[t, D_GK:, :])
        acc = acc * (1.0 / l_sel)
        half = rows // 2
        o_sel = jnp.concatenate([acc[:half, :LANES], acc[half:, LANES:]], axis=0)

        win_buf = wint_ref.shape[1]
        wt = wint_ref[...]
        wtb = wt.astype(BF16)
        wnb = wnew_t.astype(BF16)
        sw = jnp.concatenate([_dot(qbd, wtb[0:D_GK]), _dot(qbd, wnb[0:D_GK])], axis=1)
        k_pos = past - win_buf + lax.broadcasted_iota(I32, sw.shape, 1)
        dist = pos_r - k_pos
        pw = _masked_softmax(sw, (dist >= 0) & (dist < WINDOW) & (k_pos >= 0)).astype(BF16)
        accw = _dot_nt(pw[:, :win_buf], wtb[D_GK:]) + _dot_nt(pw[:, win_buf:], wnb[D_GK:])
        o_win = jnp.concatenate([accw[:half, :LANES], accw[half:, LANES:]], axis=0)

        o_ref[...] = _gate_branches(ng_ref[...], eg_ref, (o_cmp, o_sel, o_win), tq)

        shifted = pltpu.roll(wt, win_buf - tq, 1)
        newr = pltpu.roll(wnew_t, LANES - tq, 1)
        lane = lax.broadcasted_iota(I32, (D_KV, LANES), 1)
        wout_ref[:, 0:win_buf - LANES] = shifted[:, 0:win_buf - LANES]
        wout_ref[:, win_buf - LANES:] = jnp.where(lane >= LANES - tq, newr, shifted[:, win_buf - LANES:])


def _nsa_sample(nq, ng, skv, wkv, cache_cmp_t, cache_sel_t, cache_win_t, page_table, cmp_w, *, n_tok):
    db, n_pages = page_table.shape
    past = n_pages * PAGE_SIZE
    win_buf = cache_win_t.shape[2]
    ppb = 8
    tkp = ppb * PAGE_SIZE
    assert n_tok == 8 and n_tok < CMP_STRIDE and n_pages % ppb == 0 and win_buf % LANES == 0 and win_buf >= LANES
    n_st = n_pages // ppb
    nseg = past // CMP_STRIDE
    n_blocks = -(-(past + n_tok) // SEL_BLOCK)
    nb_pad = -(-n_blocks // LANES) * LANES
    ov, e3, eg = _nsa_tables(nseg, nseg - 1, n_blocks, nb_pad, (n_st + 1) * tkp, tkp)
    tri = jnp.asarray(np.arange(nb_pad)[None, :] < np.arange(nb_pad)[:, None], BF16)
    rows = NSA_HEADS * n_tok

    def page_spec(k):
        return pl.BlockSpec((None, D_KV, PAGE_SIZE), lambda b, s, pt: (pt[b, s * ppb + k], 0, 0))

    req = lambda cols: pl.BlockSpec((n_tok, cols), lambda b, s, pt: (b, 0))
    full = lambda a: pl.BlockSpec(a.shape, lambda b, s, pt: (0,) * a.ndim)
    win_spec = pl.BlockSpec((None, D_KV, win_buf), lambda b, s, pt: (b, 0, 0))
    grid_spec = pltpu.PrefetchScalarGridSpec(
        num_scalar_prefetch=1,
        grid=(db, n_st),
        in_specs=[page_spec(k) for k in range(ppb)] + [page_spec(k) for k in range(ppb)]
        + [req(D_NSA), req(LANES), req(D_KV), req(D_KV), win_spec]
        + [full(a) for a in cmp_w]
        + [full(ov), pl.BlockSpec(e3.shape, lambda b, s, pt: (0, 0, 0), pipeline_mode=pl.Buffered(1)), full(eg),
           full(tri)],
        out_specs=[req(D_NSA), win_spec],
        scratch_shapes=[pltpu.VMEM((2, NSA_GROUPS, nseg, 2 * CMP_HID), F32),
                        pltpu.VMEM((n_st, D_KV, tkp), BF16),
                        pltpu.VMEM((rows, n_st * tkp + LANES), F32),
                        pltpu.VMEM((2, 8, CMP_HID), F32)],
    )
    return pl.pallas_call(
        functools.partial(_nsa_sample_kernel, n_tok=n_tok, ppb=ppb, n_blocks=n_blocks, past=past),
        grid_spec=grid_spec,
        out_shape=[jax.ShapeDtypeStruct((db * n_tok, D_NSA), F32),
                   jax.ShapeDtypeStruct((db, D_KV, win_buf), F32)],
        compiler_params=_params("arbitrary", "arbitrary"),
        name="nsa_sample",
    )(page_table, *([cache_cmp_t] * ppb), *([cache_sel_t] * ppb), nq, ng, skv, wkv, cache_win_t,
      *cmp_w, ov, e3, eg, tri)


def _route(lg):
    lane = lax.broadcasted_iota(I32, lg.shape, 1).astype(F32)
    big = float(1 << 20)
    is_g = lane < N_GROUPS
    gl = jnp.where(is_g, lg, -jnp.inf)
    gmax = jnp.max(gl, axis=-1, keepdims=True)
    g_star = jnp.min(jnp.where(gl == gmax, lane, big), axis=-1, keepdims=True)
    g_w = 1.0 / jnp.sum(jnp.where(is_g, jnp.exp(lg - gmax), 0.0), axis=-1, keepdims=True)
    lo = N_GROUPS + g_star * EXPERTS_PER_GROUP
    el = jnp.where((lane >= lo) & (lane < lo + EXPERTS_PER_GROUP), lg, -jnp.inf)
    m1 = jnp.max(el, axis=-1, keepdims=True)
    i1 = jnp.min(jnp.where(el == m1, lane, big), axis=-1, keepdims=True)
    el2 = jnp.where(lane == i1, -jnp.inf, el)
    m2 = jnp.max(el2, axis=-1, keepdims=True)
    i2 = jnp.min(jnp.where(el2 == m2, lane, big), axis=-1, keepdims=True)
    e2 = jnp.exp(m2 - m1)
    p1 = 1.0 / (1.0 + e2)
    out = jnp.where(lane == 0, i1 - N_GROUPS, 0.0)
    out = jnp.where(lane == 1, i2 - N_GROUPS, out)
    out = jnp.where(lane == 2, g_w * p1, out)
    return jnp.where(lane == 3, g_w * (e2 * p1), out)


def _mixout_kernel(x_ref, yr_ref, on_ref, mg_ref, gt_ref, sh_ref, sc_ref, g2_ref, wr_ref, wn_ref, wo_ref,
                   wrt_ref, brt_ref, x1_ref, h2_ref, rt_ref):
    y_ret = _dot(yr_ref[...], wr_ref[...])
    y_nsa = _dot(on_ref[...].astype(BF16), wn_ref[...])
    mg = mg_ref[...].astype(F32)
    z = mg[:, :D_MODEL] * y_ret + mg[:, D_MODEL:] * y_nsa
    x1 = x_ref[...] + gt_ref[...] * _dot(z.astype(BF16), wo_ref[...])
    x1_ref[...] = x1
    h = x1 * lax.rsqrt(jnp.mean(x1 * x1, axis=-1, keepdims=True) + RMS_EPS) * g2_ref[...]
    h2 = h * (1.0 + sc_ref[...]) + sh_ref[...]
    h2_ref[...] = h2
    rt_ref[...] = _route(_dot3(h2, wrt_ref[...]) + brt_ref[...])


def _mixout(x, yret, onsa, mg, gate, shift, scale, g2, w_ret_o, w_nsa_o, w_out, w_rt, b_rt, *, tm, rows_per_mod):
    t, d = x.shape
    r = gate.shape[1]
    row = lambda cols: pl.BlockSpec((tm, cols), lambda i: (i, 0))
    mod_spec = pl.BlockSpec((None, r, d), lambda i: (i // rows_per_mod, 0, 0))
    full = lambda a: pl.BlockSpec(a.shape, lambda i: (0,) * a.ndim)
    return pl.pallas_call(
        _mixout_kernel,
        grid=(t // tm,),
        in_specs=[row(d), row(D_RET_V), row(D_NSA), row(2 * d), mod_spec, mod_spec, mod_spec,
                  pl.BlockSpec((1, d), lambda i: (0, 0)), full(w_ret_o), full(w_nsa_o), full(w_out),
                  full(w_rt), full(b_rt)],
        out_specs=[row(d), row(d), row(LANES)],
        out_shape=[jax.ShapeDtypeStruct((t, d), F32), jax.ShapeDtypeStruct((t, d), F32),
                   jax.ShapeDtypeStruct((t, LANES), F32)],
        compiler_params=_params("arbitrary"),
        name="mixout",
    )(x, yret, onsa, mg, gate, shift, scale, g2.reshape(1, d), w_ret_o, w_nsa_o, w_out, w_rt, b_rt)


def _moe_kernel(be_ref, rt_ref, nu_ref, x_hbm, w1_ref, w3_ref, w2_ref, y_ref, xbuf, sem, *, bm):
    i = pl.program_id(0)
    n_used = nu_ref[0]

    def start_rows(blk, slot):
        for r in range(bm):
            tok = rt_ref[blk * bm + r]
            pltpu.make_async_copy(x_hbm.at[pl.ds(tok, 1), :], xbuf.at[slot, pl.ds(r, 1), :], sem.at[slot]).start()

    def wait_rows(slot):
        pltpu.make_async_copy(x_hbm.at[pl.ds(0, bm), :], xbuf.at[slot], sem.at[slot]).wait()

    @pl.when((i == 0) & (n_used > 0))
    def _():
        start_rows(0, 0)

    @pl.when(i < n_used)
    def _():
        slot = i % 2
        wait_rows(slot)
        xb = xbuf[slot].astype(BF16)
        start_rows(jnp.minimum(i + 1, n_used - 1), 1 - slot)
        hid = _silu(_dot(xb, w1_ref[...])) * _dot(xb, w3_ref[...])
        y_ref[...] = _dot(hid.astype(BF16), w2_ref[...])

    @pl.when(i == n_used - 1)
    def _():
        wait_rows(1 - i % 2)

    @pl.when(i >= n_used)
    def _():
        y_ref[...] = jnp.zeros(y_ref.shape, F32)


def _moe_experts(h2, blk_expert, row_tok, n_used, w_e1, w_e3, w_e2, *, bm):
    t, d = h2.shape
    n_blk = blk_expert.shape[0]
    de = w_e1.shape[-1]
    grid_spec = pltpu.PrefetchScalarGridSpec(
        num_scalar_prefetch=3,
        grid=(n_blk,),
        in_specs=[pl.BlockSpec(memory_space=pl.ANY),
                  pl.BlockSpec((None, d, de), lambda i, be, rt, nu: (be[i], 0, 0)),
                  pl.BlockSpec((None, d, de), lambda i, be, rt, nu: (be[i], 0, 0)),
                  pl.BlockSpec((None, de, d), lambda i, be, rt, nu: (be[i], 0, 0))],
        out_specs=pl.BlockSpec((bm, d), lambda i, be, rt, nu: (i, 0)),
        scratch_shapes=[pltpu.VMEM((2, bm, d), F32), pltpu.SemaphoreType.DMA((2,))],
    )
    return pl.pallas_call(
        functools.partial(_moe_kernel, bm=bm),
        grid_spec=grid_spec,
        out_shape=jax.ShapeDtypeStruct((n_blk * bm, d), F32),
        compiler_params=_params("arbitrary"),
        name="moe_experts",
    )(blk_expert, row_tok, n_used, h2, w_e1, w_e3, w_e2)


def _dispatch_plan(expert, *, bm):
    t = expert.shape[0]
    tk = t * EXPERT_TOPK
    e_flat = expert.reshape(-1)
    onehot = (e_flat[:, None] == jnp.arange(N_EXPERTS, dtype=I32)[None, :]).astype(I32)
    csum = jnp.cumsum(onehot, axis=0)
    counts = csum[-1]
    rank = jnp.sum((csum - onehot) * onehot, axis=1)
    padded = (counts + bm - 1) // bm * bm
    pad_end = jnp.cumsum(padded)
    pad_start = pad_end - padded
    dest = pad_start[e_flat] + rank
    n_blk = -(-tk // bm) + N_EXPERTS
    blk_start = jnp.arange(n_blk, dtype=I32) * bm
    blk_expert = jnp.minimum(jnp.sum((pad_end[None, :] <= blk_start[:, None]).astype(I32), axis=1), N_EXPERTS - 1)
    row_tok = jnp.zeros((n_blk * bm,), I32).at[dest].set(jnp.arange(tk, dtype=I32) // EXPERT_TOPK)
    n_used = (pad_end[-1:] // bm).astype(I32)
    return dest.reshape(t, EXPERT_TOPK), blk_expert, row_tok, n_used


def _final_kernel(ds_ref, x1_ref, rt_ref, gt_ref, gf_ref, y_hbm, o_ref, ybuf, sem, *, tm, row0):
    i = pl.program_id(0)
    n = pl.num_programs(0)

    def start_rows(blk, slot):
        for r in range(tm):
            for k in range(EXPERT_TOPK):
                src = ds_ref[(row0 + blk * tm + r) * EXPERT_TOPK + k]
                pltpu.make_async_copy(y_hbm.at[pl.ds(src, 1), :], ybuf.at[slot, k, pl.ds(r, 1), :],
                                      sem.at[slot]).start()

    def wait_rows(slot):
        for k in range(EXPERT_TOPK):
            pltpu.make_async_copy(y_hbm.at[pl.ds(0, tm), :], ybuf.at[slot, k], sem.at[slot]).wait()

    @pl.when(i == 0)
    def _():
        start_rows(0, 0)

    slot = i % 2
    wait_rows(slot)
    y0 = ybuf[slot, 0]
    y1 = ybuf[slot, 1]
    start_rows(jnp.minimum(i + 1, n - 1), 1 - slot)
    rt = rt_ref[...]
    x2 = x1_ref[...] + gt_ref[...] * (rt[:, 2:3] * y0 + rt[:, 3:4] * y1)
    o_ref[...] = x2 * lax.rsqrt(jnp.mean(x2 * x2, axis=-1, keepdims=True) + RMS_EPS) * gf_ref[...]

    @pl.when(i == n - 1)
    def _():
        wait_rows(1 - slot)


def _final(dest_flat, x1, route, gate, g_final, y_buf, *, tm, rows_per_mod, row0):
    t, d = x1.shape
    r = gate.shape[1]
    grid_spec = pltpu.PrefetchScalarGridSpec(
        num_scalar_prefetch=1,
        grid=(t // tm,),
        in_specs=[pl.BlockSpec((tm, d), lambda i, ds: (i, 0)), pl.BlockSpec((tm, LANES), lambda i, ds: (i, 0)),
                  pl.BlockSpec((None, r, d), lambda i, ds: (i // rows_per_mod, 0, 0)),
                  pl.BlockSpec((1, d), lambda i, ds: (0, 0)), pl.BlockSpec(memory_space=pl.ANY)],
        out_specs=pl.BlockSpec((tm, d), lambda i, ds: (i, 0)),
        scratch_shapes=[pltpu.VMEM((2, EXPERT_TOPK, tm, d), F32), pltpu.SemaphoreType.DMA((2,))],
    )
    return pl.pallas_call(
        functools.partial(_final_kernel, tm=tm, row0=row0),
        grid_spec=grid_spec,
        out_shape=jax.ShapeDtypeStruct((t, d), F32),
        compiler_params=_params("arbitrary"),
        name="final",
    )(dest_flat, x1, route, gate, g_final.reshape(1, d), y_buf)


def kernel(x_prompt, x_sample, c_prompt, c_sample, state_ret, cache_cmp_kv, cache_sel_kv, cache_win_kv, page_table,
           w_ada, b_ada, g_norm1, g_norm2, g_final, w_in, ret_norm_g, cmp_w1, cmp_pe, cmp_w2, w_ret_o, w_nsa_o,
           w_out, w_grp, b_grp, w_exp, b_exp, w_e1, w_e3, w_e2):
    depth = w_in.shape[0]
    assert depth == 1, "one layer"
    batch, seq, d = x_prompt.shape
    db, n_tok, _ = x_sample.shape
    n_pages = page_table.shape[1]
    past = n_pages * PAGE_SIZE
    win_buf = cache_win_kv.shape[2]
    assert win_buf <= seq
    tp = batch * seq
    ts = db * n_tok
    l = 0

    mod = _adaln(jnp.concatenate([c_prompt, c_sample], axis=0).astype(F32), w_ada[l], b_ada[l])
    mods = jnp.split(mod, 6, axis=-1)
    mod_p = [m[:batch].reshape(batch, 1, d) for m in mods]
    ts_tile = min(ts, 256)
    mod_s = [jnp.repeat(m[batch:], n_tok, axis=0).reshape(ts // ts_tile, ts_tile, d) for m in mods]

    w_pad = _pad_w_in(w_in[l])
    w_t = w_in[l][:, C_NQ:C_NG].T.astype(BF16)
    cmp_w = _cmp_weights(cmp_w1[l], cmp_pe[l], cmp_w2[l])
    wr, wn, wo = w_ret_o[l].astype(BF16), w_nsa_o[l].astype(BF16), w_out[l].astype(BF16)
    w_rt = jnp.concatenate([w_grp[l], w_exp[l], jnp.zeros((d, LANES - N_GROUPS - N_EXPERTS), F32)], axis=1)
    b_rt = jnp.concatenate([b_grp[l], b_exp[l], jnp.zeros((LANES - N_GROUPS - N_EXPERTS,), F32)]).reshape(1, LANES)

    tm_p = 256
    cos_p, sin_p = _rope_tables(jnp.arange(seq))
    (rq, rk, rv, rg, nqt, ckv_pt, skv_pt, wkv_pt, skb, wkb, ng, mg, svt, wvt) = _inproj(
        x_prompt.reshape(tp, d).astype(F32), mod_p[0], mod_p[1], g_norm1[l], cos_p, sin_p, w_pad, w_t,
        tm=tm_p, rows_per_mod=seq // tm_p, pos_blocks=seq // tm_p, act_dtype=BF16)
    yret_p, ret_state_p = _ret_prompt(rq, rk, rv, rg, ret_norm_g[l], batch=batch, seq=seq)
    ck, cvt = _cmp_prompt(ckv_pt, cmp_w, batch=batch, seq=seq)
    onsa_p = _nsa_prompt(nqt, ng, ck, cvt, skb, svt, wkb, wvt, batch=batch, seq=seq)
    tm_m = 512
    x1_p, h2_p, route_p = _mixout(x_prompt.reshape(tp, d).astype(F32), yret_p, onsa_p, mg, mod_p[2], mod_p[3],
                                  mod_p[4], g_norm2[l], wr, wn, wo, w_rt, b_rt, tm=tm_m, rows_per_mod=seq // tm_m)

    cos_s, sin_s = _rope_tables(past + jnp.arange(n_tok))
    cos_s = jnp.tile(cos_s, (ts_tile // n_tok, 1))
    sin_s = jnp.tile(sin_s, (ts_tile // n_tok, 1))
    (rq_s, rk_s, rv_s, rg_s, nq_s, ckv_s, skv_s, wkv_s, ng_s, mg_s) = _inproj(
        x_sample.reshape(ts, d).astype(F32), mod_s[0], mod_s[1], g_norm1[l], cos_s, sin_s, w_pad, None,
        tm=ts_tile, rows_per_mod=1, pos_blocks=1, act_dtype=F32)
    yret_s, ret_state_s = _ret_sample(rq_s, rk_s, rv_s, rg_s, state_ret[l].astype(F32), ret_norm_g[l], n_tok=n_tok)

    def feature_major(cache):
        n, r = cache.shape[:2]
        return jnp.transpose(cache, (0, 2, 3, 4, 1)).reshape(n, D_KV, r).astype(F32)

    onsa_s, win_s_t = _nsa_sample(nq_s, ng_s, skv_s, wkv_s, feature_major(cache_cmp_kv[l]),
                                  feature_major(cache_sel_kv[l]), feature_major(cache_win_kv[l]), page_table, cmp_w,
                                  n_tok=n_tok)
    def token_major(a_t):
        n, _, r = a_t.shape
        return jnp.transpose(a_t.reshape(n, 2, NSA_GROUPS, NSA_HEAD_DIM, r), (0, 4, 1, 2, 3))[None]

    x1_s, h2_s, route_s = _mixout(x_sample.reshape(ts, d).astype(F32), yret_s, onsa_s, mg_s, mod_s[2], mod_s[3],
                                  mod_s[4], g_norm2[l], wr, wn, wo, w_rt, b_rt, tm=ts_tile, rows_per_mod=1)

    bm = 256
    h2 = jnp.concatenate([h2_p, h2_s], axis=0)
    route = jnp.concatenate([route_p, route_s], axis=0)
    expert = route[:, :EXPERT_TOPK].astype(I32)
    dest, blk_expert, row_tok, n_used = _dispatch_plan(expert, bm=bm)
    y_buf = _moe_experts(h2, blk_expert, row_tok, n_used,
                         w_e1[l].astype(BF16), w_e3[l].astype(BF16), w_e2[l].astype(BF16), bm=bm)
    dest_flat = dest.reshape(-1)
    y_p = _final(dest_flat, x1_p, route_p, mod_p[5], g_final, y_buf, tm=tm_p, rows_per_mod=seq // tm_p, row0=0)
    y_s = _final(dest_flat, x1_s, route_s, mod_s[5], g_final, y_buf, tm=ts_tile, rows_per_mod=1, row0=tp)

    kv6 = lambda a, n, s: a.reshape(1, n, s, 2, NSA_GROUPS, NSA_HEAD_DIM)
    return (y_p.reshape(batch, seq, d).astype(x_prompt.dtype), y_s.reshape(db, n_tok, d).astype(x_sample.dtype),
            ret_state_p[None].astype(state_ret.dtype), ret_state_s[None].astype(state_ret.dtype),
            token_major(ckv_pt).astype(cache_cmp_kv.dtype), kv6(ckv_s, db, n_tok).astype(cache_cmp_kv.dtype),
            token_major(skv_pt).astype(cache_sel_kv.dtype), kv6(skv_s, db, n_tok).astype(cache_sel_kv.dtype),
            token_major(wkv_pt[:, :, seq - win_buf:]).astype(cache_win_kv.dtype),
            token_major(win_s_t).astype(cache_win_kv.dtype))
```
